```python
import jax
import jax.numpy as jnp
from jax import lax
import numpy as np

D_MODEL = 1024
BATCH = 4
SEQ = 4096
DEPTH = 2
DEC_BATCH = 128
DEC_SEQ = 4
PAST_LEN = 8192
PAGE_SIZE = 128

N_META = 16
HEAD_DIM = 64
N_Q_HEADS = 8
N_KV_HEADS = 2
GROUP = N_Q_HEADS // N_KV_HEADS
WINDOW = 128
BLOCK = 128
ROPE_THETA = 10000.0
CONV_DIM = 512
CONV_W = 3
POOL_WINDOWS = (2, 4, 8, 16)
POOL_GROUPS = len(POOL_WINDOWS)
POOL_GC = D_MODEL // POOL_GROUPS
POOL_HIST = max(POOL_WINDOWS) - 1
D_FF = 2816
RMS_EPS = 1e-6
N_EVEN = (DEPTH + 1) // 2
N_ODD = DEPTH // 2
ATT_W = N_Q_HEADS * HEAD_DIM
KV_W = N_KV_HEADS * HEAD_DIM
IN_W = ATT_W + 2 * KV_W + 3 * CONV_DIM
MIX_W = ATT_W + CONV_DIM
SPLITS = (ATT_W, ATT_W + KV_W, ATT_W + 2 * KV_W, ATT_W + 2 * KV_W + CONV_DIM, ATT_W + 2 * KV_W + 2 * CONV_DIM)

kernel_name = "hybrid_swa_sink_shortconv_pool_macaron_step"


def rmsnorm(x, g):
    xf = x.astype(jnp.float32)
    y = xf * lax.rsqrt(jnp.mean(xf * xf, axis=-1, keepdims=True) + RMS_EPS)
    return (y * g.astype(jnp.float32)).astype(x.dtype)


def swiglu(x, w_gate, w_up, w_down):
    return (jax.nn.silu(x @ w_gate) * (x @ w_up)) @ w_down


def rope(x, pos):
    half = HEAD_DIM // 2
    inv = ROPE_THETA ** (-jnp.arange(half, dtype=jnp.float32) / half)
    ang = pos.astype(jnp.float32)[:, None] * inv[None, :]
    cos = jnp.cos(ang)[None, :, None, :]
    sin = jnp.sin(ang)[None, :, None, :]
    xf = x.astype(jnp.float32)
    x1, x2 = xf[..., :half], xf[..., half:]
    return jnp.concatenate([x1 * cos - x2 * sin, x2 * cos + x1 * sin], axis=-1).astype(x.dtype)


def sink_softmax(s, mask, sink):
    s = jnp.where(mask, s, -jnp.inf)
    sk = sink.astype(jnp.float32)[..., None, None]
    m = jnp.maximum(jnp.max(s, axis=-1, keepdims=True), sk)
    p = jnp.exp(s - m)
    return p / (jnp.sum(p, axis=-1, keepdims=True) + jnp.exp(sk - m))


def swa_prompt(q, k, v, sink):
    b, t = q.shape[:2]
    pad = (-t) % BLOCK
    nb = (t + pad) // BLOCK
    front = lambda a: jnp.pad(a, ((0, 0), (pad, 0)) + ((0, 0),) * (a.ndim - 2))
    qb = front(q).reshape(b, nb, BLOCK, N_KV_HEADS, GROUP, HEAD_DIM)
    kb = front(k).reshape(b, nb, BLOCK, N_KV_HEADS, HEAD_DIM)
    vb = front(v).reshape(b, nb, BLOCK, N_KV_HEADS, HEAD_DIM)
    band = lambda a: jnp.concatenate([jnp.pad(a, ((0, 0), (1, 0), (0, 0), (0, 0), (0, 0)))[:, :-1], a], axis=2)
    kband, vband = band(kb), band(vb)
    s = jnp.einsum("bnqhgd,bnkhd->bnhgqk", qb, kband, preferred_element_type=jnp.float32) * (HEAD_DIM ** -0.5)
    start = jnp.arange(nb, dtype=jnp.int32)[:, None] * BLOCK - pad
    qpos = start + jnp.arange(BLOCK, dtype=jnp.int32)[None, :]
    kpos = start - BLOCK + jnp.arange(2 * BLOCK, dtype=jnp.int32)[None, :]
    dist = qpos[:, :, None] - kpos[:, None, :]
    mask = (kpos[:, None, :] >= 0) & (dist >= 0) & (dist <= WINDOW)
    p = sink_softmax(s, mask[None, :, None, None], sink)
    o = jnp.einsum("bnhgqk,bnkhd->bnqhgd", p.astype(v.dtype), vband)
    return o.reshape(b, nb * BLOCK, ATT_W)[:, pad:]


def swa_sample(q, k, v, k_buf, v_buf, sink, pos):
    bd, s_len = q.shape[:2]
    w = k_buf.shape[1]
    kk = jnp.concatenate([k_buf, k], axis=1)
    vv = jnp.concatenate([v_buf, v], axis=1)
    qg = q.reshape(bd, s_len, N_KV_HEADS, GROUP, HEAD_DIM)
    s = jnp.einsum("bqhgd,bkhd->bhgqk", qg, kk, preferred_element_type=jnp.float32) * (HEAD_DIM ** -0.5)
    kpos = pos[0] - w + jnp.arange(w + s_len, dtype=jnp.int32)
    dist = pos[:, None] - kpos[None, :]
    mask = (dist >= 0) & (dist <= WINDOW)
    p = sink_softmax(s, mask, sink)
    o = jnp.einsum("bhgqk,bkhd->bqhgd", p.astype(vv.dtype), vv).reshape(bd, s_len, ATT_W)
    return o, kk[:, -w:], vv[:, -w:]


def causal_dwconv(u_ext, w, n_out):
    out = w[0] * u_ext[:, 0:n_out]
    for j in range(1, CONV_W):
        out = out + w[j] * u_ext[:, j:j + n_out]
    return out


def pool_mixer(u_ext, pos_ext, n_out, w_group, scale):
    b, l = u_ext.shape[:2]
    uf = u_ext.astype(jnp.float32)
    csum = jnp.pad(jnp.cumsum(uf, axis=1), ((0, 0), (1, 0), (0, 0)))
    outs = []
    for gi, win in enumerate(POOL_WINDOWS):
        cg = csum[..., gi * POOL_GC:(gi + 1) * POOL_GC]
        shifted = jnp.pad(cg, ((0, 0), (win, 0), (0, 0)))[:, :l + 1]
        wsum = cg[:, 1:] - shifted[:, 1:]
        cnt = jnp.minimum(win, pos_ext + 1).astype(jnp.float32)
        outs.append(wsum / cnt[None, :, None] - uf[..., gi * POOL_GC:(gi + 1) * POOL_GC])
    p = jnp.stack(outs, axis=2)[:, -n_out:].astype(u_ext.dtype)
    z = jnp.einsum("blgc,gce->blge", p, w_group).reshape(b, n_out, D_MODEL)
    return z * scale


def run_trunk(x, pos, past, ln_gain, ffn_w_gate, ffn_w_up, ffn_w_down, mix_w_in, attn_sink, conv_w, mix_w_out, pool_w, pool_scale, final_gain):
    b, l = x.shape[:2]
    k_rows, v_rows, conv_rows, pool_rows = [], [], [], []
    for layer in range(DEPTH):
        g = ln_gain[layer]
        x = x + 0.5 * swiglu(rmsnorm(x, g[0]), ffn_w_gate[layer, 0], ffn_w_up[layer, 0], ffn_w_down[layer, 0])
        h = rmsnorm(x, g[1])
        if layer % 2 == 0:
            i = layer // 2
            q, k, v, gate_b, gate_c, hc = jnp.split(h @ mix_w_in[i], SPLITS, axis=-1)
            q = rope(q.reshape(b, l, N_Q_HEADS, HEAD_DIM), pos)
            k = rope(k.reshape(b, l, N_KV_HEADS, HEAD_DIM), pos)
            v = v.reshape(b, l, N_KV_HEADS, HEAD_DIM)
            if past is None:
                att = swa_prompt(q, k, v, attn_sink[i])
                keep = min(WINDOW, PAST_LEN)
                k_keep, v_keep = k[:, -keep:], v[:, -keep:]
                conv_hist = jnp.zeros((b, CONV_W - 1, CONV_DIM), h.dtype)
            else:
                att, k_keep, v_keep = swa_sample(q, k, v, past[0][i], past[1][i], attn_sink[i], pos)
                conv_hist = past[2][i]
            u_ext = jnp.concatenate([conv_hist.astype(h.dtype), gate_c * hc], axis=1)
            conv = causal_dwconv(u_ext, conv_w[i], l)
            mix = jnp.concatenate([att, gate_b * conv], axis=-1) @ mix_w_out[i]
            k_rows.append(k_keep)
            v_rows.append(v_keep)
            conv_rows.append(u_ext[:, -(CONV_W - 1):])
        else:
            j = layer // 2
            if past is None:
                u_ext, pos_ext = h, pos
            else:
                u_ext = jnp.concatenate([past[3][j].astype(h.dtype), h], axis=1)
                pos_ext = jnp.concatenate([pos[0] - POOL_HIST + jnp.arange(POOL_HIST, dtype=jnp.int32), pos])
            mix = pool_mixer(u_ext, pos_ext, l, pool_w[j], pool_scale[j])
            pool_rows.append(u_ext[:, -POOL_HIST:])
        x = x + mix
        x = x + 0.5 * swiglu(rmsnorm(x, g[2]), ffn_w_gate[layer, 1], ffn_w_up[layer, 1], ffn_w_down[layer, 1])
    return rmsnorm(x, final_gain), jnp.stack(k_rows), jnp.stack(v_rows), jnp.stack(conv_rows), jnp.stack(pool_rows)


def setup_inputs(seed: int = 0) -> dict:
    key = jax.random.key(seed)
    ks = jax.random.split(key, 18)
    nrm = lambda k, shape, scale: scale * jax.random.normal(k, shape, jnp.float32)
    winb = min(WINDOW, PAST_LEN)
    return {
        "x_prompt": nrm(ks[0], (BATCH, SEQ, D_MODEL), 1.0),
        "x_sample": nrm(ks[1], (DEC_BATCH, DEC_SEQ, D_MODEL), 1.0),
        "cache_k": nrm(ks[2], (N_EVEN, DEC_BATCH, winb, N_KV_HEADS, HEAD_DIM), 1.0),
        "cache_v": nrm(ks[3], (N_EVEN, DEC_BATCH, winb, N_KV_HEADS, HEAD_DIM), 1.0),
        "state_conv": nrm(ks[4], (N_EVEN, DEC_BATCH, CONV_W - 1, CONV_DIM), 1.0),
        "state_pool": nrm(ks[5], (N_ODD, DEC_BATCH, POOL_HIST, D_MODEL), 1.0),
        "meta_tokens": nrm(ks[6], (N_META, D_MODEL), 1.0),
        "ln_gain": 1.0 + nrm(ks[7], (DEPTH, 3, D_MODEL), 0.02),
        "ffn_w_gate": nrm(ks[8], (DEPTH, 2, D_MODEL, D_FF), D_MODEL ** -0.5),
        "ffn_w_up": nrm(ks[9], (DEPTH, 2, D_MODEL, D_FF), D_MODEL ** -0.5),
        "ffn_w_down": nrm(ks[10], (DEPTH, 2, D_FF, D_MODEL), D_FF ** -0.5),
        "mix_w_in": nrm(ks[11], (N_EVEN, D_MODEL, IN_W), D_MODEL ** -0.5),
        "attn_sink": nrm(ks[12], (N_EVEN, N_KV_HEADS, GROUP), 1.0),
        "conv_w": nrm(ks[13], (N_EVEN, CONV_W, CONV_DIM), CONV_W ** -0.5),
        "mix_w_out": nrm(ks[14], (N_EVEN, MIX_W, D_MODEL), MIX_W ** -0.5),
        "pool_w": nrm(ks[15], (N_ODD, POOL_GROUPS, POOL_GC, POOL_GC), POOL_GC ** -0.5),
        "pool_scale": 1.0 + nrm(ks[16], (N_ODD, D_MODEL), 0.02),
        "final_gain": 1.0 + nrm(ks[17], (D_MODEL,), 0.02),
    }


def reference(x_prompt, x_sample, cache_k, cache_v, state_conv, state_pool, meta_tokens, ln_gain, ffn_w_gate, ffn_w_up, ffn_w_down, mix_w_in, attn_sink, conv_w, mix_w_out, pool_w, pool_scale, final_gain):
    weights = (ln_gain, ffn_w_gate, ffn_w_up, ffn_w_down, mix_w_in, attn_sink, conv_w, mix_w_out, pool_w, pool_scale, final_gain)
    b = x_prompt.shape[0]
    meta = jnp.broadcast_to(meta_tokens.astype(x_prompt.dtype)[None], (b, N_META, D_MODEL))
    xp = jnp.concatenate([meta, x_prompt], axis=1)
    pos_p = jnp.arange(xp.shape[1], dtype=jnp.int32)
    y_full, k_p, v_p, conv_p, pool_p = run_trunk(xp, pos_p, None, *weights)
    y_prompt = y_full[:, N_META:]
    pos_s = PAST_LEN + jnp.arange(x_sample.shape[1], dtype=jnp.int32)
    y_sample, k_s, v_s, conv_s, pool_s = run_trunk(x_sample, pos_s, (cache_k, cache_v, state_conv, state_pool), *weights)
    return (y_prompt, y_sample, k_p, v_p, conv_p, pool_p, k_s, v_s, conv_s, pool_s)
```

```python
import functools

import jax
import jax.numpy as jnp
from jax import lax
from jax.experimental import pallas as pl
from jax.experimental.pallas import tpu as pltpu

F32 = jnp.float32
BF16 = jnp.bfloat16

D_MODEL = 1024
N_META = 16
HEAD_DIM = 64
N_Q_HEADS = 8
N_KV_HEADS = 2
GROUP = N_Q_HEADS // N_KV_HEADS
WINDOW = 128
BLOCK = 128
ROPE_THETA = 10000.0
CONV_DIM = 512
CONV_W = 3
POOL_WINDOWS = (2, 4, 8, 16)
POOL_GC = D_MODEL // len(POOL_WINDOWS)
POOL_HIST = max(POOL_WINDOWS) - 1
D_FF = 2816
RMS_EPS = 1e-6
PAST_LEN = 8192
ATT_W = N_Q_HEADS * HEAD_DIM
KV_W = N_KV_HEADS * HEAD_DIM
IN_W = ATT_W + 2 * KV_W + 3 * CONV_DIM

V7X_VMEM_BYTES = 64 * 1024 * 1024
VMEM_LIMIT = V7X_VMEM_BYTES * 7 // 8
LANES = 128
META_TILE = BLOCK
META_PAD = META_TILE - N_META
HALO_U = 8
HALO_P = 16


def _params():
    return pltpu.CompilerParams(vmem_limit_bytes=VMEM_LIMIT)


def _const_spec(shape):
    nd = len(shape)
    return pl.BlockSpec(shape, lambda *_: (0,) * nd, pipeline_mode=pl.Buffered(1))


def _rms(x, g):
    ms = jnp.mean(x * x, axis=-1, keepdims=True)
    return (x * lax.rsqrt(ms + RMS_EPS)) * g


def _ffn_kernel(x_ref, g_ref, wg_ref, wu_ref, wd_ref, *rest, final):
    if final:
        fg_ref, o_ref = rest
    else:
        (o_ref,) = rest
    x = x_ref[...]
    h = _rms(x, g_ref[...]).astype(BF16)
    gate = jnp.dot(h, wg_ref[...], preferred_element_type=F32)
    up = jnp.dot(h, wu_ref[...], preferred_element_type=F32)
    act = ((gate * (1.0 / (1.0 + jnp.exp(-gate)))) * up).astype(BF16)
    y = x + 0.5 * jnp.dot(act, wd_ref[...], preferred_element_type=F32)
    if final:
        y = _rms(y, fg_ref[...])
    o_ref[...] = y


def _ffn_call(x, gain, wg, wu, wd, tm, final_gain=None):
    rows = x.shape[0]
    assert rows % tm == 0
    final = final_gain is not None
    row_spec = pl.BlockSpec((tm, D_MODEL), lambda i: (i, 0))
    in_specs = [row_spec, _const_spec((1, D_MODEL)), _const_spec((D_MODEL, D_FF)),
                _const_spec((D_MODEL, D_FF)), _const_spec((D_FF, D_MODEL))]
    args = [x, gain.reshape(1, D_MODEL), wg, wu, wd]
    if final:
        in_specs.append(_const_spec((1, D_MODEL)))
        args.append(final_gain.reshape(1, D_MODEL))
    return pl.pallas_call(
        functools.partial(_ffn_kernel, final=final),
        grid=(rows // tm,),
        in_specs=in_specs,
        out_specs=row_spec,
        out_shape=jax.ShapeDtypeStruct((rows, D_MODEL), F32),
        compiler_params=_params(),
        name="ffn_final" if final else "ffn",
    )(*args)


def _rope(x, cos, sin_signed):
    w = x.shape[1]
    lane = lax.broadcasted_iota(jnp.int32, x.shape, 1)
    first_half = (lane % HEAD_DIM) < (HEAD_DIM // 2)
    partner = jnp.where(first_half, pltpu.roll(x, w - HEAD_DIM // 2, 1), pltpu.roll(x, HEAD_DIM // 2, 1))
    reps = w // LANES
    if reps > 1:
        cos = jnp.concatenate([cos] * reps, axis=1)
        sin_signed = jnp.concatenate([sin_signed] * reps, axis=1)
    return x * cos + partner * sin_signed


def _inproj_kernel(x_ref, g_ref, w_ref, cos_ref, sin_ref, q_ref, k_ref, v_ref, gb_ref, u_ref):
    h = _rms(x_ref[...], g_ref[...]).astype(BF16)
    p = jnp.dot(h, w_ref[...], preferred_element_type=F32)
    cos = cos_ref[...]
    sin_signed = sin_ref[...]
    o = 0
    q = _rope(p[:, o:o + ATT_W], cos, sin_signed) * (HEAD_DIM ** -0.5)
    o += ATT_W
    k = _rope(p[:, o:o + KV_W], cos, sin_signed)
    o += KV_W
    v = p[:, o:o + KV_W]
    o += KV_W
    gb = p[:, o:o + CONV_DIM]
    o += CONV_DIM
    gc = p[:, o:o + CONV_DIM]
    o += CONV_DIM
    hc = p[:, o:o + CONV_DIM]
    q_ref[...] = q.astype(q_ref.dtype)
    k_ref[...] = k
    v_ref[...] = v
    gb_ref[...] = gb
    u_ref[...] = gc * hc


def _inproj_call(x, gain, w_in, cos, sin_signed, tm, table_blocks):
    rows = x.shape[0]
    assert rows % tm == 0
    row = lambda w: pl.BlockSpec((tm, w), lambda i: (i, 0))
    tab = pl.BlockSpec((tm, LANES), lambda i: (i % table_blocks, 0))
    shapes = [(ATT_W, BF16), (KV_W, F32), (KV_W, F32), (CONV_DIM, F32), (CONV_DIM, F32)]
    return pl.pallas_call(
        _inproj_kernel,
        grid=(rows // tm,),
        in_specs=[row(D_MODEL), _const_spec((1, D_MODEL)), _const_spec((D_MODEL, IN_W)), tab, tab],
        out_specs=[row(w) for w, _ in shapes],
        out_shape=[jax.ShapeDtypeStruct((rows, w), dt) for w, dt in shapes],
        compiler_params=_params(),
        name="inproj",
    )(x, gain.reshape(1, D_MODEL), w_in, cos, sin_signed)


def _softmax_pv(s, bias, sinks, vv):
    r, kk = bias.shape
    lane = lax.broadcasted_iota(jnp.int32, (r, LANES), 1)
    p_rows, inv = [], []
    for c in range(GROUP):
        p_halves, inv_halves = [], []
        for h in range(N_KV_HEADS):
            sb = s[c * r:(c + 1) * r, h * kk:(h + 1) * kk] + bias
            snk = sinks[h][c]
            m = jnp.maximum(jnp.max(sb, axis=1, keepdims=True), snk)
            p = jnp.exp(sb - m)
            l = jnp.sum(p, axis=1, keepdims=True) + jnp.exp(snk - m)
            p_halves.append(p.astype(BF16))
            inv_halves.append(1.0 / l)
        p_rows.append(jnp.concatenate(p_halves, axis=1))
        inv.append(jnp.where(lane < HEAD_DIM, inv_halves[0], inv_halves[1]))
    o = jnp.dot(jnp.concatenate(p_rows, axis=0), vv, preferred_element_type=F32)
    return [o[c * r:(c + 1) * r] * inv[c] for c in range(GROUP)]


def _mix0_kernel(sink_ref, x_ref, q_ref, k_ref, v_ref, gb_ref, u_ref, hk_ref, hv_ref, hu_ref, cw_ref, wout_ref,
                 o_ref, ulast_ref, k0h, k1h, v0h, v1h, ubuf, *, tm, kmin_first):
    i = pl.program_id(1)
    lane_h = lax.broadcasted_iota(jnp.int32, (BLOCK, LANES), 1) < HEAD_DIM

    @pl.when(i == 0)
    def _():
        hk = hk_ref[...]
        hv = hv_ref[...]
        k0h[...] = jnp.where(lane_h, hk, 0.0).astype(BF16)
        k1h[...] = jnp.where(lane_h, 0.0, hk).astype(BF16)
        v0h[...] = jnp.where(lane_h, hv, 0.0).astype(BF16)
        v1h[...] = jnp.where(lane_h, 0.0, hv).astype(BF16)
        ubuf[0:HALO_U, :] = hu_ref[...]

    lane_t = lax.broadcasted_iota(jnp.int32, (tm, LANES), 1) < HEAD_DIM
    kf = k_ref[...]
    vf = v_ref[...]
    k0 = jnp.concatenate([k0h[...], jnp.where(lane_t, kf, 0.0).astype(BF16)], axis=0)
    k1 = jnp.concatenate([k1h[...], jnp.where(lane_t, 0.0, kf).astype(BF16)], axis=0)
    v0 = jnp.concatenate([v0h[...], jnp.where(lane_t, vf, 0.0).astype(BF16)], axis=0)
    v1 = jnp.concatenate([v1h[...], jnp.where(lane_t, 0.0, vf).astype(BF16)], axis=0)
    k0h[...] = k0[tm:tm + BLOCK]
    k1h[...] = k1[tm:tm + BLOCK]
    v0h[...] = v0[tm:tm + BLOCK]
    v1h[...] = v1[tm:tm + BLOCK]

    sinks = [[sink_ref[h * GROUP + c] for c in range(GROUP)] for h in range(N_KV_HEADS)]
    row = lax.broadcasted_iota(jnp.int32, (BLOCK, 2 * BLOCK), 0)
    col = lax.broadcasted_iota(jnp.int32, (BLOCK, 2 * BLOCK), 1)
    band = (col >= row) & (col <= row + WINDOW)
    bias_rest = jnp.where(band, 0.0, -jnp.inf)
    kmin = jnp.where(i == 0, kmin_first, 0)
    bias_first = jnp.where(band & (col >= kmin), 0.0, -jnp.inf)

    q = q_ref[...]
    att_blocks = []
    for j in range(tm // BLOCK):
        ks = slice(j * BLOCK, (j + 2) * BLOCK)
        kk = jnp.concatenate([k0[ks], k1[ks]], axis=0)
        vv = jnp.concatenate([v0[ks], v1[ks]], axis=0)
        qs = jnp.concatenate([q[j * BLOCK:(j + 1) * BLOCK, c * LANES:(c + 1) * LANES] for c in range(GROUP)], axis=0)
        s = lax.dot_general(qs, kk, (((1,), (1,)), ((), ())), preferred_element_type=F32)
        o = _softmax_pv(s, bias_first if j == 0 else bias_rest, sinks, vv)
        att_blocks.append(jnp.concatenate(o, axis=1))
    att = jnp.concatenate(att_blocks, axis=0) if len(att_blocks) > 1 else att_blocks[0]

    u = u_ref[...]
    ubuf[HALO_U:HALO_U + tm, :] = u
    cw = cw_ref[...]
    conv = cw[0:1] * ubuf[HALO_U - 2:HALO_U - 2 + tm, :] + cw[1:2] * ubuf[HALO_U - 1:HALO_U - 1 + tm, :] + cw[2:3] * u
    u_tail = u[tm - HALO_U:tm]
    ubuf[0:HALO_U, :] = u_tail
    ulast_ref[...] = u_tail

    mix_in = jnp.concatenate([att, gb_ref[...] * conv], axis=1).astype(BF16)
    o_ref[...] = x_ref[...] + jnp.dot(mix_in, wout_ref[...], preferred_element_type=F32)


def _mix0_call(sink, x, q, k, v, gb, u, hk, hv, hu, cw, wout, *, nbatch, tm, kmin_first, rows):
    per = rows // nbatch
    assert per % tm == 0 and tm % BLOCK == 0
    tpb = per // tm
    row = lambda w: pl.BlockSpec((tm, w), lambda b, i: (b * tpb + i, 0))
    const = lambda shape: _const_spec(shape)
    return pl.pallas_call(
        functools.partial(_mix0_kernel, tm=tm, kmin_first=kmin_first),
        grid=(nbatch, tpb),
        in_specs=[pl.BlockSpec(memory_space=pltpu.SMEM),
                  row(D_MODEL), row(ATT_W), row(KV_W), row(KV_W), row(CONV_DIM), row(CONV_DIM),
                  const((BLOCK, KV_W)), const((BLOCK, KV_W)), const((HALO_U, CONV_DIM)),
                  const((CONV_W, CONV_DIM)), const((D_MODEL, D_MODEL))],
        out_specs=[row(D_MODEL), pl.BlockSpec((HALO_U, CONV_DIM), lambda b, i: (b, 0))],
        out_shape=[jax.ShapeDtypeStruct((rows, D_MODEL), F32),
                   jax.ShapeDtypeStruct((nbatch * HALO_U, CONV_DIM), F32)],
        scratch_shapes=[pltpu.VMEM((BLOCK, KV_W), BF16)] * 4 + [pltpu.VMEM((tm + HALO_U, CONV_DIM), F32)],
        compiler_params=_params(),
        name="mix0",
    )(sink, x, q, k, v, gb, u, hk, hv, hu, cw, wout)


def _mix0s_kernel(sink_ref, x_ref, q_ref, k_ref, v_ref, gb_ref, u_ref, ck_ref, cv_ref, sc_ref, cw_ref, wout_ref,
                  o_ref, ko_ref, vo_ref, co_ref, qs_scr, os_scr, kext, vext, *, sb, ns):
    ext = WINDOW + 8
    lane_h = lax.broadcasted_iota(jnp.int32, (sb, LANES), 1) < HEAD_DIM
    for t in range(ns):
        qt = q_ref[t].astype(F32)
        for c in range(GROUP):
            chunk = qt[:, c * LANES:(c + 1) * LANES]
            qs_scr[(0 * GROUP + c) * ns + t] = jnp.where(lane_h, chunk, 0.0)
            qs_scr[(1 * GROUP + c) * ns + t] = jnp.where(lane_h, 0.0, chunk)
    nrow = N_KV_HEADS * GROUP * ns
    rid = lax.broadcasted_iota(jnp.int32, (nrow, ext), 0)
    e = lax.broadcasted_iota(jnp.int32, (nrow, ext), 1)
    t_of = rid % ns
    bias = jnp.where((e >= t_of) & (e <= t_of + WINDOW), 0.0, -jnp.inf)
    rid1 = lax.broadcasted_iota(jnp.int32, (nrow, 1), 0)
    snk = jnp.zeros((nrow, 1), F32)
    for hc in range(N_KV_HEADS * GROUP):
        snk = jnp.where(rid1 // ns == hc, sink_ref[hc], snk)
    kext[WINDOW:ext, :] = jnp.zeros((ext - WINDOW, LANES), F32)
    vext[WINDOW:ext, :] = jnp.zeros((ext - WINDOW, LANES), F32)
    for b in range(sb):
        kext[0:WINDOW, :] = ck_ref[b]
        kext[WINDOW:WINDOW + ns, :] = k_ref[:, b, :]
        vext[0:WINDOW, :] = cv_ref[b]
        vext[WINDOW:WINDOW + ns, :] = v_ref[:, b, :]
        ko_ref[b] = kext[ns:ns + WINDOW, :]
        vo_ref[b] = vext[ns:ns + WINDOW, :]
        qb = qs_scr[:, b, :].astype(BF16)
        s = lax.dot_general(qb, kext[...].astype(BF16), (((1,), (1,)), ((), ())), preferred_element_type=F32) + bias
        m = jnp.maximum(jnp.max(s, axis=1, keepdims=True), snk)
        p = jnp.exp(s - m)
        l = jnp.sum(p, axis=1, keepdims=True) + jnp.exp(snk - m)
        o = jnp.dot(p.astype(BF16), vext[...].astype(BF16), preferred_element_type=F32)
        os_scr[:, b, :] = o * (1.0 / l)
    cw = cw_ref[...]
    ue = [sc_ref[:, r, :] for r in range(CONV_W - 1)] + [u_ref[t] for t in range(ns)]
    rows = []
    for t in range(ns):
        att = jnp.concatenate(
            [jnp.where(lane_h, os_scr[(0 * GROUP + c) * ns + t], os_scr[(1 * GROUP + c) * ns + t]) for c in range(GROUP)],
            axis=1)
        conv = cw[0:1] * ue[t] + cw[1:2] * ue[t + 1] + cw[2:3] * ue[t + 2]
        rows.append(jnp.concatenate([att, gb_ref[t] * conv], axis=1))
    mix_in = jnp.concatenate(rows, axis=0).astype(BF16)
    y = jnp.dot(mix_in, wout_ref[...], preferred_element_type=F32)
    for t in range(ns):
        o_ref[t] = x_ref[t] + y[t * sb:(t + 1) * sb]
    for r in range(CONV_W - 1):
        co_ref[:, r, :] = ue[ns + r]


def _mix0s_call(sink, x, q, k, v, gb, u, ck, cv, sc, cw, wout, *, sb):
    ns, nseq = x.shape[0], x.shape[1]
    assert nseq % sb == 0
    tmaj = lambda w: pl.BlockSpec((ns, sb, w), lambda i: (0, i, 0))
    cache = pl.BlockSpec((sb, WINDOW, KV_W), lambda i: (i, 0, 0))
    cstate = pl.BlockSpec((sb, CONV_W - 1, CONV_DIM), lambda i: (i, 0, 0))
    nrow = N_KV_HEADS * GROUP * ns
    return pl.pallas_call(
        functools.partial(_mix0s_kernel, sb=sb, ns=ns),
        grid=(nseq // sb,),
        in_specs=[pl.BlockSpec(memory_space=pltpu.SMEM),
                  tmaj(D_MODEL), tmaj(ATT_W), tmaj(KV_W), tmaj(KV_W), tmaj(CONV_DIM), tmaj(CONV_DIM),
                  cache, cache, cstate, _const_spec((CONV_W, CONV_DIM)), _const_spec((D_MODEL, D_MODEL))],
        out_specs=[tmaj(D_MODEL), cache, cache, cstate],
        out_shape=[jax.ShapeDtypeStruct((ns, nseq, D_MODEL), F32),
                   jax.ShapeDtypeStruct((nseq, WINDOW, KV_W), F32),
                   jax.ShapeDtypeStruct((nseq, WINDOW, KV_W), F32),
                   jax.ShapeDtypeStruct((nseq, CONV_W - 1, CONV_DIM), F32)],
        scratch_shapes=[pltpu.VMEM((nrow, sb, LANES), F32), pltpu.VMEM((nrow, sb, LANES), F32),
                        pltpu.VMEM((WINDOW + 8, KV_W), F32), pltpu.VMEM((WINDOW + 8, KV_W), F32)],
        compiler_params=_params(),
        name="mix0_sample",
    )(sink, x, q, k, v, gb, u, ck, cv, sc, cw, wout)


def _group_linear(p_groups, pw_ref, ps, x):
    z = jnp.concatenate(
        [jnp.dot(p.astype(BF16), pw_ref[g], preferred_element_type=F32) for g, p in enumerate(p_groups)], axis=1)
    return x + z * ps


def _pool_kernel(x_ref, g_ref, hh_ref, pw_ref, ps_ref, o_ref, hlast_ref, hbuf, *, tm, pos0):
    i = pl.program_id(1)

    @pl.when(i == 0)
    def _():
        hbuf[0:HALO_P, :] = hh_ref[...]

    x = x_ref[...]
    h = _rms(x, g_ref[...])
    hbuf[HALO_P:HALO_P + tm, :] = h
    pos = pos0 + i * tm + lax.broadcasted_iota(jnp.int32, (tm, 1), 0)
    p_groups = []
    for g, win in enumerate(POOL_WINDOWS):
        lanes = slice(g * POOL_GC, (g + 1) * POOL_GC)
        hg = h[:, lanes]
        wsum = hg
        for d in range(1, win):
            wsum = wsum + hbuf[HALO_P - d:HALO_P - d + tm, lanes]
        cnt = jnp.clip(pos + 1, 1, win).astype(F32)
        p_groups.append(wsum * (1.0 / cnt) - hg)
    o_ref[...] = _group_linear(p_groups, pw_ref, ps_ref[...], x)
    tail = h[tm - HALO_P:tm]
    hbuf[0:HALO_P, :] = tail
    hlast_ref[...] = tail


def _pool_call(x, gain, hh, pw, ps, *, nbatch, tm, pos0, rows):
    per = rows // nbatch
    assert per % tm == 0
    tpb = per // tm
    row = pl.BlockSpec((tm, D_MODEL), lambda b, i: (b * tpb + i, 0))
    ng = len(POOL_WINDOWS)
    return pl.pallas_call(
        functools.partial(_pool_kernel, tm=tm, pos0=pos0),
        grid=(nbatch, tpb),
        in_specs=[row, _const_spec((1, D_MODEL)), _const_spec((HALO_P, D_MODEL)),
                  _const_spec((ng, POOL_GC, POOL_GC)), _const_spec((1, D_MODEL))],
        out_specs=[row, pl.BlockSpec((HALO_P, D_MODEL), lambda b, i: (b, 0))],
        out_shape=[jax.ShapeDtypeStruct((rows, D_MODEL), F32),
                   jax.ShapeDtypeStruct((nbatch * HALO_P, D_MODEL), F32)],
        scratch_shapes=[pltpu.VMEM((tm + HALO_P, D_MODEL), F32)],
        compiler_params=_params(),
        name="pool",
    )(x, gain.reshape(1, D_MODEL), hh, pw, ps.reshape(1, D_MODEL))


def _pools_kernel(x_ref, g_ref, sp_ref, pw_ref, ps_ref, o_ref, po_ref, *, ns):
    g = g_ref[...]
    xs = [x_ref[t] for t in range(ns)]
    ext = [sp_ref[:, r, :] for r in range(POOL_HIST)] + [_rms(x, g) for x in xs]
    inv = [1.0 / win for win in POOL_WINDOWS]
    for t in range(ns):
        p_groups = []
        for gi, win in enumerate(POOL_WINDOWS):
            lanes = slice(gi * POOL_GC, (gi + 1) * POOL_GC)
            hg = ext[POOL_HIST + t][:, lanes]
            wsum = hg
            for d in range(1, win):
                wsum = wsum + ext[POOL_HIST + t - d][:, lanes]
            p_groups.append(wsum * inv[gi] - hg)
        o_ref[t] = _group_linear(p_groups, pw_ref, ps_ref[...], xs[t])
    for r in range(POOL_HIST):
        po_ref[:, r, :] = ext[ns + r]


def _pools_call(x, gain, sp, pw, ps, *, sb):
    ns, nseq = x.shape[0], x.shape[1]
    assert PAST_LEN + 1 >= max(POOL_WINDOWS) and nseq % sb == 0
    tmaj = pl.BlockSpec((ns, sb, D_MODEL), lambda i: (0, i, 0))
    state = pl.BlockSpec((sb, POOL_HIST, D_MODEL), lambda i: (i, 0, 0))
    ng = len(POOL_WINDOWS)
    return pl.pallas_call(
        functools.partial(_pools_kernel, ns=ns),
        grid=(nseq // sb,),
        in_specs=[tmaj, _const_spec((1, D_MODEL)), state, _const_spec((ng, POOL_GC, POOL_GC)),
                  _const_spec((1, D_MODEL))],
        out_specs=[tmaj, state],
        out_shape=[jax.ShapeDtypeStruct((ns, nseq, D_MODEL), F32),
                   jax.ShapeDtypeStruct((nseq, POOL_HIST, D_MODEL), F32)],
        compiler_params=_params(),
        name="pool_sample",
    )(x, gain.reshape(1, D_MODEL), sp, pw, ps.reshape(1, D_MODEL))


def _rope_tables(pos):
    half = HEAD_DIM // 2
    inv = ROPE_THETA ** (-jnp.arange(half, dtype=F32) / half)
    ang = pos.astype(F32)[:, None] * inv[None, :]
    cos, sin = jnp.cos(ang), jnp.sin(ang)
    reps = LANES // HEAD_DIM
    cos_t = jnp.concatenate([cos, cos] * reps, axis=1)
    sin_t = jnp.concatenate([-sin, sin] * reps, axis=1)
    return cos_t, sin_t


def _head_perm():
    idx = []
    for c in range(GROUP):
        idx += list(range(c * HEAD_DIM, (c + 1) * HEAD_DIM))
        idx += list(range((GROUP + c) * HEAD_DIM, (GROUP + c + 1) * HEAD_DIM))
    return jnp.asarray(idx, dtype=jnp.int32)


def kernel(x_prompt, x_sample, cache_k, cache_v, state_conv, state_pool, meta_tokens, ln_gain, ffn_w_gate, ffn_w_up,
           ffn_w_down, mix_w_in, attn_sink, conv_w, mix_w_out, pool_w, pool_scale, final_gain):
    nb, seq, _ = x_prompt.shape
    nseq, ns, _ = x_sample.shape
    assert ln_gain.shape[0] == 2 and meta_tokens.shape[0] == N_META and cache_k.shape[2] == WINDOW
    tm = 512
    small_tm = META_TILE
    srows = META_TILE + ns * nseq
    sb = 32

    perm = _head_perm()
    w_in = mix_w_in[0]
    w_in = jnp.concatenate([w_in[:, :ATT_W][:, perm], w_in[:, ATT_W:]], axis=1).astype(BF16)
    w_out = mix_w_out[0]
    w_out = jnp.concatenate([w_out[:ATT_W][perm], w_out[ATT_W:]], axis=0).astype(BF16)
    wg, wu, wd = ffn_w_gate.astype(BF16), ffn_w_up.astype(BF16), ffn_w_down.astype(BF16)
    pw = pool_w[0].astype(BF16)
    sink = attn_sink[0].reshape(N_Q_HEADS)
    cw = conv_w[0]

    cos_p, sin_p = _rope_tables(N_META + jnp.arange(seq, dtype=jnp.int32))
    pos_small = jnp.concatenate([jnp.arange(META_TILE, dtype=jnp.int32) - META_PAD,
                                 jnp.repeat(PAST_LEN + jnp.arange(ns, dtype=jnp.int32), nseq)])
    cos_s, sin_s = _rope_tables(pos_small)

    xp = x_prompt.reshape(nb * seq, D_MODEL)
    xs = jnp.concatenate([jnp.zeros((META_PAD, D_MODEL), F32), meta_tokens.astype(F32),
                          jnp.transpose(x_sample, (1, 0, 2)).reshape(ns * nseq, D_MODEL)], axis=0)

    def tmajor(a):
        return a[META_TILE:].reshape(ns, nseq, a.shape[1])

    xs = _ffn_call(xs, ln_gain[0, 0], wg[0, 0], wu[0, 0], wd[0, 0], small_tm)
    xp = _ffn_call(xp, ln_gain[0, 0], wg[0, 0], wu[0, 0], wd[0, 0], tm)
    qs, ks, vs, gbs, us = _inproj_call(xs, ln_gain[0, 1], w_in, cos_s, sin_s, small_tm, srows // small_tm)
    qp, kp, vp, gbp, up = _inproj_call(xp, ln_gain[0, 1], w_in, cos_p, sin_p, tm, seq // tm)

    zk = jnp.zeros((BLOCK, KV_W), F32)
    zu = jnp.zeros((HALO_U, CONV_DIM), F32)
    xm, um_last = _mix0_call(sink, xs, qs, ks, vs, gbs, us, zk, zk, zu, cw, w_out,
                             nbatch=1, tm=META_TILE, kmin_first=2 * BLOCK - N_META, rows=META_TILE)
    xp, up_last = _mix0_call(sink, xp, qp, kp, vp, gbp, up, ks[:META_TILE], vs[:META_TILE], um_last, cw, w_out,
                             nbatch=nb, tm=tm, kmin_first=BLOCK - N_META, rows=nb * seq)
    xss, k_s, v_s, conv_s = _mix0s_call(
        sink, tmajor(xs), tmajor(qs), tmajor(ks), tmajor(vs), tmajor(gbs), tmajor(us),
        cache_k[0].reshape(nseq, WINDOW, KV_W), cache_v[0].reshape(nseq, WINDOW, KV_W), state_conv[0], cw, w_out, sb=sb)
    xs = jnp.concatenate([xm, xss.reshape(ns * nseq, D_MODEL)], axis=0)

    xs = _ffn_call(xs, ln_gain[0, 2], wg[0, 1], wu[0, 1], wd[0, 1], small_tm)
    xp = _ffn_call(xp, ln_gain[0, 2], wg[0, 1], wu[0, 1], wd[0, 1], tm)

    xs = _ffn_call(xs, ln_gain[1, 0], wg[1, 0], wu[1, 0], wd[1, 0], small_tm)
    xp = _ffn_call(xp, ln_gain[1, 0], wg[1, 0], wu[1, 0], wd[1, 0], tm)

    zh = jnp.zeros((HALO_P, D_MODEL), F32)
    xm, hm_last = _pool_call(xs, ln_gain[1, 1], zh, pw, pool_scale[0], nbatch=1, tm=META_TILE, pos0=-META_PAD,
                             rows=META_TILE)
    xp, hp_last = _pool_call(xp, ln_gain[1, 1], hm_last, pw, pool_scale[0], nbatch=nb, tm=tm, pos0=N_META,
                             rows=nb * seq)
    xss, pool_s = _pools_call(tmajor(xs), ln_gain[1, 1], state_pool[0], pw, pool_scale[0], sb=sb)
    xs = jnp.concatenate([xm, xss.reshape(ns * nseq, D_MODEL)], axis=0)

    xs = _ffn_call(xs, ln_gain[1, 2], wg[1, 1], wu[1, 1], wd[1, 1], small_tm, final_gain=final_gain)
    xp = _ffn_call(xp, ln_gain[1, 2], wg[1, 1], wu[1, 1], wd[1, 1], tm, final_gain=final_gain)

    y_prompt = xp.reshape(nb, seq, D_MODEL)
    y_sample = jnp.transpose(xs[META_TILE:].reshape(ns, nseq, D_MODEL), (1, 0, 2))
    kv_shape = (1, nb, WINDOW, N_KV_HEADS, HEAD_DIM)
    k_prompt = kp.reshape(nb, seq, KV_W)[:, seq - WINDOW:].reshape(kv_shape)
    v_prompt = vp.reshape(nb, seq, KV_W)[:, seq - WINDOW:].reshape(kv_shape)
    conv_prompt = up_last.reshape(1, nb, HALO_U, CONV_DIM)[:, :, HALO_U - (CONV_W - 1):]
    pool_prompt = hp_last.reshape(1, nb, HALO_P, D_MODEL)[:, :, HALO_P - POOL_HIST:]
    skv_shape = (1, nseq, WINDOW, N_KV_HEADS, HEAD_DIM)
    return (y_prompt, y_sample, k_prompt, v_prompt, conv_prompt, pool_prompt,
            k_s.reshape(skv_shape), v_s.reshape(skv_shape), conv_s[None], pool_s[None])
```

```python
import dataclasses
import functools

import jax
import jax.numpy as jnp
from jax import lax
from jax.experimental import pallas as pl
from jax.experimental.pallas import tpu as pltpu

F32 = jnp.float32
BF16 = jnp.bfloat16

D_MODEL = 1024
N_META = 16
HEAD_DIM = 64
N_Q_HEADS = 8
N_KV_HEADS = 2
GROUP = N_Q_HEADS // N_KV_HEADS
WINDOW = 128
BLOCK = 128
ROPE_THETA = 10000.0
CONV_DIM = 512
CONV_W = 3
POOL_WINDOWS = (2, 4, 8, 16)
POOL_GC = D_MODEL // len(POOL_WINDOWS)
POOL_HIST = max(POOL_WINDOWS) - 1
D_FF = 2816
RMS_EPS = 1e-6
PAST_LEN = 8192
ATT_W = N_Q_HEADS * HEAD_DIM
KV_W = N_KV_HEADS * HEAD_DIM
IN_W = ATT_W + 2 * KV_W + 3 * CONV_DIM

V7X_VMEM_BYTES = 64 * 1024 * 1024
VMEM_LIMIT = V7X_VMEM_BYTES * 7 // 8
LANES = 128
META_TILE = BLOCK
META_PAD = META_TILE - N_META
HALO_U = 8
HALO_P = 16
TM = 512
SEQ_BLOCK = 32
NEW_PAD = 8


def _params():
    return pltpu.CompilerParams(vmem_limit_bytes=VMEM_LIMIT)


def _const_spec(shape):
    nd = len(shape)
    return pl.BlockSpec(shape, lambda *_: (0,) * nd, pipeline_mode=pl.Buffered(1))


def _rms(x, g):
    ms = jnp.mean(x * x, axis=-1, keepdims=True)
    return (x * lax.rsqrt(ms + RMS_EPS)) * g


def _ffn_body(x, g, wg_ref, wu_ref, wd_ref):
    h = _rms(x, g).astype(BF16)
    gate = jnp.dot(h, wg_ref[...], preferred_element_type=F32)
    up = jnp.dot(h, wu_ref[...], preferred_element_type=F32)
    act = ((gate * (1.0 / (1.0 + jnp.exp(-gate)))) * up).astype(BF16)
    return x + 0.5 * jnp.dot(act, wd_ref[...], preferred_element_type=F32)


def _rope(x, cos, sin_signed):
    w = x.shape[1]
    lane = lax.broadcasted_iota(jnp.int32, x.shape, 1)
    first_half = (lane % HEAD_DIM) < (HEAD_DIM // 2)
    partner = jnp.where(first_half, pltpu.roll(x, w - HEAD_DIM // 2, 1), pltpu.roll(x, HEAD_DIM // 2, 1))
    reps = w // LANES
    if reps > 1:
        cos = jnp.concatenate([cos] * reps, axis=1)
        sin_signed = jnp.concatenate([sin_signed] * reps, axis=1)
    return x * cos + partner * sin_signed


def _inproj_body(x, g, w_ref, cos, sin_signed):
    h = _rms(x, g).astype(BF16)
    p = jnp.dot(h, w_ref[...], preferred_element_type=F32)
    o = 0
    q = _rope(p[:, o:o + ATT_W], cos, sin_signed) * (HEAD_DIM ** -0.5)
    o += ATT_W
    k = _rope(p[:, o:o + KV_W], cos, sin_signed)
    o += KV_W
    v = p[:, o:o + KV_W]
    o += KV_W
    gb = p[:, o:o + CONV_DIM]
    o += CONV_DIM
    gc = p[:, o:o + CONV_DIM]
    o += CONV_DIM
    hc = p[:, o:o + CONV_DIM]
    return q, k, v, gb, gc * hc


PROJ_OUT = ((ATT_W, BF16), (KV_W, F32), (KV_W, F32), (CONV_DIM, F32), (CONV_DIM, F32))


@dataclasses.dataclass(frozen=True)
class _StageCfg:
    n_main: int
    sample_in: str = ""
    sample_out: bool = False
    inproj: bool = False
    final: bool = False
    ns: int = 4
    nseq: int = 128


def _stage_kernel(*refs, cfg):
    it = iter(refs)
    xm_ref, xa_ref = next(it), next(it)
    xs_ref = next(it) if cfg.sample_in else None
    g_ref, wg_ref, wu_ref, wd_ref = next(it), next(it), next(it), next(it)
    if cfg.inproj:
        g1_ref, win_ref, cosm_ref, sinm_ref, cosa_ref, sina_ref = (next(it) for _ in range(6))
    fg_ref = next(it) if cfg.final else None
    om_ref, oa_ref = next(it), next(it)
    os_ref = next(it) if cfg.sample_out else None
    if cfg.inproj:
        pm_refs = [next(it) for _ in PROJ_OUT]
        pa_refs = [next(it) for _ in PROJ_OUT]
    x_scr = next(it) if cfg.sample_in else None
    y_scr = next(it) if cfg.sample_out else None

    i = pl.program_id(0)
    first_sample = cfg.n_main + 1 if cfg.sample_in else cfg.n_main

    def run(x, o_ref, tab_refs, p_refs):
        y = _ffn_body(x, g_ref[...], wg_ref, wu_ref, wd_ref)
        if cfg.final:
            y = _rms(y, fg_ref[...])
        o_ref[...] = y
        if cfg.inproj:
            outs = _inproj_body(y, g1_ref[...], win_ref, tab_refs[0][...], tab_refs[1][...])
            for r, val in zip(p_refs, outs):
                r[...] = val.astype(r.dtype)

    @pl.when(i == 0)
    def _():
        run(xm_ref[...], om_ref, (cosm_ref, sinm_ref) if cfg.inproj else None, pm_refs if cfg.inproj else None)

    if cfg.sample_in:
        @pl.when((i >= 1) & (i < first_sample))
        def _():
            x_scr[...] = xa_ref[...]

        @pl.when(i == first_sample)
        def _():
            if cfg.sample_in == "seq":
                for t in range(cfg.ns):
                    x_scr[t * cfg.nseq:(t + 1) * cfg.nseq, :] = xs_ref[:, t, :]
            else:
                x_scr[...] = xs_ref[...]

    @pl.when(i >= 1)
    def _():
        x = x_scr[...] if cfg.sample_in else xa_ref[...]
        run(x, y_scr if cfg.sample_out else oa_ref, (cosa_ref, sina_ref) if cfg.inproj else None,
            pa_refs if cfg.inproj else None)

    if cfg.sample_out:
        @pl.when((i >= 1) & (i < first_sample))
        def _():
            oa_ref[...] = y_scr[...]

        @pl.when(i == first_sample)
        def _():
            for t in range(cfg.ns):
                os_ref[:, t, :] = y_scr[t * cfg.nseq:(t + 1) * cfg.nseq, :]


def _stage_call(xm, xa, xs, gain, w_ffn, layer, which, *, cfg, proj=None, final_gain=None, name="stage"):
    wg, wu, wd = w_ffn
    n_in = cfg.n_main
    n_out = cfg.n_main if cfg.sample_out else (cfg.n_main + 1 if cfg.sample_in else cfg.n_main)
    steps = 1 + (n_in + 1 if cfg.sample_in else n_in)
    assert xa.shape[0] == n_in * TM and xm.shape[0] == META_TILE

    def main_spec(w, n):
        return pl.BlockSpec((TM, w), lambda i: (jnp.clip(i - 1, 0, n - 1), 0))

    meta_spec = lambda w: _const_spec((META_TILE, w))
    wsel = lambda shape: pl.BlockSpec((None, None) + shape, lambda i: (layer, which, 0, 0),
                                      pipeline_mode=pl.Buffered(1))
    in_specs = [meta_spec(D_MODEL), main_spec(D_MODEL, n_in)]
    args = [xm, xa]
    if cfg.sample_in:
        in_specs.append(_const_spec(xs.shape))
        args.append(xs)
    in_specs += [_const_spec((1, D_MODEL)), wsel((D_MODEL, D_FF)), wsel((D_MODEL, D_FF)), wsel((D_FF, D_MODEL))]
    args += [gain.reshape(1, D_MODEL), wg, wu, wd]
    if cfg.inproj:
        g1, w_in, cos_m, sin_m, cos_a, sin_a, tpb = proj
        n_tab = cos_a.shape[0] // TM
        tab = pl.BlockSpec((TM, LANES), lambda i: (jnp.where(i - 1 < cfg.n_main, jnp.maximum(i - 1, 0) % tpb, n_tab - 1), 0))
        in_specs += [_const_spec((1, D_MODEL)), _const_spec((D_MODEL, IN_W)), meta_spec(LANES), meta_spec(LANES), tab, tab]
        args += [g1.reshape(1, D_MODEL), w_in, cos_m, sin_m, cos_a, sin_a]
    if cfg.final:
        in_specs.append(_const_spec((1, D_MODEL)))
        args.append(final_gain.reshape(1, D_MODEL))

    out_specs = [meta_spec(D_MODEL), main_spec(D_MODEL, n_out)]
    out_shape = [jax.ShapeDtypeStruct((META_TILE, D_MODEL), F32), jax.ShapeDtypeStruct((n_out * TM, D_MODEL), F32)]
    if cfg.sample_out:
        out_specs.append(_const_spec((cfg.nseq, cfg.ns, D_MODEL)))
        out_shape.append(jax.ShapeDtypeStruct((cfg.nseq, cfg.ns, D_MODEL), F32))
    if cfg.inproj:
        out_specs += [meta_spec(w) for w, _ in PROJ_OUT] + [main_spec(w, n_out) for w, _ in PROJ_OUT]
        out_shape += [jax.ShapeDtypeStruct((META_TILE, w), dt) for w, dt in PROJ_OUT]
        out_shape += [jax.ShapeDtypeStruct((n_out * TM, w), dt) for w, dt in PROJ_OUT]
    scratch = []
    if cfg.sample_in:
        scratch.append(pltpu.VMEM((TM, D_MODEL), F32))
    if cfg.sample_out:
        scratch.append(pltpu.VMEM((TM, D_MODEL), F32))
    return pl.pallas_call(
        functools.partial(_stage_kernel, cfg=cfg),
        grid=(steps,),
        in_specs=in_specs,
        out_specs=out_specs,
        out_shape=out_shape,
        scratch_shapes=scratch,
        compiler_params=_params(),
        name=name,
    )(*args)


def _softmax_pv(s, bias, sinks, vv):
    r, kk = bias.shape
    lane = lax.broadcasted_iota(jnp.int32, (r, LANES), 1)
    p_rows, inv = [], []
    for c in range(GROUP):
        p_halves, inv_halves = [], []
        for h in range(N_KV_HEADS):
            sb = s[c * r:(c + 1) * r, h * kk:(h + 1) * kk] + bias
            snk = sinks[h][c]
            m = jnp.maximum(jnp.max(sb, axis=1, keepdims=True), snk)
            p = jnp.exp(sb - m)
            l = jnp.sum(p, axis=1, keepdims=True) + jnp.exp(snk - m)
            p_halves.append(p.astype(BF16))
            inv_halves.append(1.0 / l)
        p_rows.append(jnp.concatenate(p_halves, axis=1))
        inv.append(jnp.where(lane < HEAD_DIM, inv_halves[0], inv_halves[1]))
    o = jnp.dot(jnp.concatenate(p_rows, axis=0), vv, preferred_element_type=F32)
    return [o[c * r:(c + 1) * r] * inv[c] for c in range(GROUP)]


def _mix0_kernel(sink_ref, x_ref, q_ref, k_ref, v_ref, gb_ref, u_ref, hk_ref, hv_ref, hu_ref, cw_ref, wout_ref,
                 o_ref, ulast_ref, k0h, k1h, v0h, v1h, ubuf, *, tm, kmin_first):
    i = pl.program_id(1)
    lane_h = lax.broadcasted_iota(jnp.int32, (BLOCK, LANES), 1) < HEAD_DIM

    @pl.when(i == 0)
    def _():
        hk = hk_ref[...]
        hv = hv_ref[...]
        k0h[...] = jnp.where(lane_h, hk, 0.0).astype(BF16)
        k1h[...] = jnp.where(lane_h, 0.0, hk).astype(BF16)
        v0h[...] = jnp.where(lane_h, hv, 0.0).astype(BF16)
        v1h[...] = jnp.where(lane_h, 0.0, hv).astype(BF16)
        ubuf[0:HALO_U, :] = hu_ref[...]

    lane_t = lax.broadcasted_iota(jnp.int32, (tm, LANES), 1) < HEAD_DIM
    kf = k_ref[...]
    vf = v_ref[...]
    k0 = jnp.concatenate([k0h[...], jnp.where(lane_t, kf, 0.0).astype(BF16)], axis=0)
    k1 = jnp.concatenate([k1h[...], jnp.where(lane_t, 0.0, kf).astype(BF16)], axis=0)
    v0 = jnp.concatenate([v0h[...], jnp.where(lane_t, vf, 0.0).astype(BF16)], axis=0)
    v1 = jnp.concatenate([v1h[...], jnp.where(lane_t, 0.0, vf).astype(BF16)], axis=0)
    k0h[...] = k0[tm:tm + BLOCK]
    k1h[...] = k1[tm:tm + BLOCK]
    v0h[...] = v0[tm:tm + BLOCK]
    v1h[...] = v1[tm:tm + BLOCK]

    sinks = [[sink_ref[h * GROUP + c] for c in range(GROUP)] for h in range(N_KV_HEADS)]
    row = lax.broadcasted_iota(jnp.int32, (BLOCK, 2 * BLOCK), 0)
    col = lax.broadcasted_iota(jnp.int32, (BLOCK, 2 * BLOCK), 1)
    band = (col >= row) & (col <= row + WINDOW)
    bias_rest = jnp.where(band, 0.0, -jnp.inf)
    kmin = jnp.where(i == 0, kmin_first, 0)
    bias_first = jnp.where(band & (col >= kmin), 0.0, -jnp.inf)

    q = q_ref[...]
    att_blocks = []
    for j in range(tm // BLOCK):
        ks = slice(j * BLOCK, (j + 2) * BLOCK)
        kk = jnp.concatenate([k0[ks], k1[ks]], axis=0)
        vv = jnp.concatenate([v0[ks], v1[ks]], axis=0)
        qs = jnp.concatenate([q[j * BLOCK:(j + 1) * BLOCK, c * LANES:(c + 1) * LANES] for c in range(GROUP)], axis=0)
        s = lax.dot_general(qs, kk, (((1,), (1,)), ((), ())), preferred_element_type=F32)
        o = _softmax_pv(s, bias_first if j == 0 else bias_rest, sinks, vv)
        att_blocks.append(jnp.concatenate(o, axis=1))
    att = jnp.concatenate(att_blocks, axis=0) if len(att_blocks) > 1 else att_blocks[0]

    u = u_ref[...]
    ubuf[HALO_U:HALO_U + tm, :] = u
    cw = cw_ref[...]
    conv = cw[0:1] * ubuf[HALO_U - 2:HALO_U - 2 + tm, :] + cw[1:2] * ubuf[HALO_U - 1:HALO_U - 1 + tm, :] + cw[2:3] * u
    u_tail = u[tm - HALO_U:tm]
    ubuf[0:HALO_U, :] = u_tail
    ulast_ref[...] = u_tail

    mix_in = jnp.concatenate([att, gb_ref[...] * conv], axis=1).astype(BF16)
    o_ref[...] = x_ref[...] + jnp.dot(mix_in, wout_ref[...], preferred_element_type=F32)


def _mix0_call(sink, x, q, k, v, gb, u, hk, hv, hu, cw, wout, *, nbatch, tm, kmin_first, rows):
    per = rows // nbatch
    assert per % tm == 0 and tm % BLOCK == 0
    tpb = per // tm
    row = lambda w: pl.BlockSpec((tm, w), lambda b, i: (b * tpb + i, 0))
    return pl.pallas_call(
        functools.partial(_mix0_kernel, tm=tm, kmin_first=kmin_first),
        grid=(nbatch, tpb),
        in_specs=[pl.BlockSpec(memory_space=pltpu.SMEM),
                  row(D_MODEL), row(ATT_W), row(KV_W), row(KV_W), row(CONV_DIM), row(CONV_DIM),
                  _const_spec((BLOCK, KV_W)), _const_spec((BLOCK, KV_W)), _const_spec((HALO_U, CONV_DIM)),
                  _const_spec((CONV_W, CONV_DIM)), _const_spec((D_MODEL, D_MODEL))],
        out_specs=[row(D_MODEL), pl.BlockSpec((HALO_U, CONV_DIM), lambda b, i: (b, 0))],
        out_shape=[jax.ShapeDtypeStruct((rows, D_MODEL), F32),
                   jax.ShapeDtypeStruct((nbatch * HALO_U, CONV_DIM), F32)],
        scratch_shapes=[pltpu.VMEM((BLOCK, KV_W), BF16)] * 4 + [pltpu.VMEM((tm + HALO_U, CONV_DIM), F32)],
        compiler_params=_params(),
        name="mix0",
    )(sink, x, q, k, v, gb, u, hk, hv, hu, cw, wout)


def _mix0s_kernel(sink_ref, x_ref, q_ref, k_ref, v_ref, gb_ref, u_ref, ck_ref, cv_ref, sc_ref, cw_ref, wout_ref,
                  o_ref, ko_ref, vo_ref, co_ref, qs_scr, os_scr, knew, vnew, *, sb, ns, nseq):
    ext = WINDOW + NEW_PAD
    base = pl.multiple_of(pl.program_id(0) * sb, sb)
    rows_t = lambda ref, t: ref[pl.ds(t * nseq + base, sb), :]
    lane_h = lax.broadcasted_iota(jnp.int32, (sb, LANES), 1) < HEAD_DIM
    for t in range(ns):
        qt = rows_t(q_ref, t).astype(F32)
        for c in range(GROUP):
            chunk = qt[:, c * LANES:(c + 1) * LANES]
            qs_scr[(0 * GROUP + c) * ns + t] = jnp.where(lane_h, chunk, 0.0)
            qs_scr[(1 * GROUP + c) * ns + t] = jnp.where(lane_h, 0.0, chunk)
        knew[t] = rows_t(k_ref, t)
        vnew[t] = rows_t(v_ref, t)
    nrow = N_KV_HEADS * GROUP * ns
    rid = lax.broadcasted_iota(jnp.int32, (nrow, ext), 0)
    e = lax.broadcasted_iota(jnp.int32, (nrow, ext), 1)
    t_of = rid % ns
    bias = jnp.where((e >= t_of) & (e <= t_of + WINDOW), 0.0, -jnp.inf)
    rid1 = lax.broadcasted_iota(jnp.int32, (nrow, 1), 0)
    snk = jnp.zeros((nrow, 1), F32)
    for hc in range(N_KV_HEADS * GROUP):
        snk = jnp.where(rid1 // ns == hc, sink_ref[hc], snk)
    zrows = jnp.zeros((NEW_PAD - ns, sb, LANES), F32)
    knew[ns:NEW_PAD] = zrows
    vnew[ns:NEW_PAD] = zrows
    for b in range(sb):
        kb = jnp.concatenate([ck_ref[b], knew[:, b, :]], axis=0)
        vb = jnp.concatenate([cv_ref[b], vnew[:, b, :]], axis=0)
        ko_ref[b] = kb[ns:ns + WINDOW]
        vo_ref[b] = vb[ns:ns + WINDOW]
        qb = qs_scr[:, b, :].astype(BF16)
        s = lax.dot_general(qb, kb.astype(BF16), (((1,), (1,)), ((), ())), preferred_element_type=F32) + bias
        m = jnp.maximum(jnp.max(s, axis=1, keepdims=True), snk)
        p = jnp.exp(s - m)
        l = jnp.sum(p, axis=1, keepdims=True) + jnp.exp(snk - m)
        o = jnp.dot(p.astype(BF16), vb.astype(BF16), preferred_element_type=F32)
        os_scr[:, b, :] = o * (1.0 / l)
    cw = cw_ref[...]
    ue = [sc_ref[:, r, :] for r in range(CONV_W - 1)] + [rows_t(u_ref, t) for t in range(ns)]
    rows = []
    for t in range(ns):
        att = jnp.concatenate(
            [jnp.where(lane_h, os_scr[(0 * GROUP + c) * ns + t], os_scr[(1 * GROUP + c) * ns + t]) for c in range(GROUP)],
            axis=1)
        conv = cw[0:1] * ue[t] + cw[1:2] * ue[t + 1] + cw[2:3] * ue[t + 2]
        rows.append(jnp.concatenate([att, rows_t(gb_ref, t) * conv], axis=1))
    mix_in = jnp.concatenate(rows, axis=0).astype(BF16)
    y = jnp.dot(mix_in, wout_ref[...], preferred_element_type=F32)
    for t in range(ns):
        o_ref[pl.ds(t * nseq + base, sb), :] = rows_t(x_ref, t) + y[t * sb:(t + 1) * sb]
    for r in range(CONV_W - 1):
        co_ref[:, r, :] = ue[ns + r]


def _mix0s_call(sink, x, q, k, v, gb, u, ck, cv, sc, cw, wout, *, sb, ns, tile):
    nseq = ck.shape[0]
    assert nseq % sb == 0 and ns * nseq == TM
    res = lambda w: pl.BlockSpec((TM, w), lambda i: (tile, 0), pipeline_mode=pl.Buffered(1))
    cache = pl.BlockSpec((sb, WINDOW, KV_W), lambda i: (i, 0, 0))
    cstate = pl.BlockSpec((sb, CONV_W - 1, CONV_DIM), lambda i: (i, 0, 0))
    nrow = N_KV_HEADS * GROUP * ns
    return pl.pallas_call(
        functools.partial(_mix0s_kernel, sb=sb, ns=ns, nseq=nseq),
        grid=(nseq // sb,),
        in_specs=[pl.BlockSpec(memory_space=pltpu.SMEM),
                  res(D_MODEL), res(ATT_W), res(KV_W), res(KV_W), res(CONV_DIM), res(CONV_DIM),
                  cache, cache, cstate, _const_spec((CONV_W, CONV_DIM)), _const_spec((D_MODEL, D_MODEL))],
        out_specs=[pl.BlockSpec((TM, D_MODEL), lambda i: (0, 0)), cache, cache, cstate],
        out_shape=[jax.ShapeDtypeStruct((TM, D_MODEL), F32),
                   jax.ShapeDtypeStruct((nseq, WINDOW, KV_W), F32),
                   jax.ShapeDtypeStruct((nseq, WINDOW, KV_W), F32),
                   jax.ShapeDtypeStruct((nseq, CONV_W - 1, CONV_DIM), F32)],
        scratch_shapes=[pltpu.VMEM((nrow, sb, LANES), F32), pltpu.VMEM((nrow, sb, LANES), F32),
                        pltpu.VMEM((NEW_PAD, sb, KV_W), F32), pltpu.VMEM((NEW_PAD, sb, KV_W), F32)],
        compiler_params=_params(),
        name="mix0_sample",
    )(sink, x, q, k, v, gb, u, ck, cv, sc, cw, wout)


def _group_linear(p_groups, pw_ref, ps, x):
    z = jnp.concatenate(
        [jnp.dot(p.astype(BF16), pw_ref[g], preferred_element_type=F32) for g, p in enumerate(p_groups)], axis=1)
    return x + z * ps


def _pool_kernel(x_ref, g_ref, hh_ref, pw_ref, ps_ref, o_ref, hlast_ref, hbuf, *, tm, pos0):
    i = pl.program_id(1)

    @pl.when(i == 0)
    def _():
        hbuf[0:HALO_P, :] = hh_ref[...]

    x = x_ref[...]
    h = _rms(x, g_ref[...])
    hbuf[HALO_P:HALO_P + tm, :] = h
    pos = pos0 + i * tm + lax.broadcasted_iota(jnp.int32, (tm, 1), 0)
    p_groups = []
    for g, win in enumerate(POOL_WINDOWS):
        lanes = slice(g * POOL_GC, (g + 1) * POOL_GC)
        hg = h[:, lanes]
        wsum = hg
        for d in range(1, win):
            wsum = wsum + hbuf[HALO_P - d:HALO_P - d + tm, lanes]
        cnt = jnp.clip(pos + 1, 1, win).astype(F32)
        p_groups.append(wsum * (1.0 / cnt) - hg)
    o_ref[...] = _group_linear(p_groups, pw_ref, ps_ref[...], x)
    tail = h[tm - HALO_P:tm]
    hbuf[0:HALO_P, :] = tail
    hlast_ref[...] = tail


def _pool_call(x, gain, hh, pw, ps, *, nbatch, tm, pos0, rows):
    per = rows // nbatch
    assert per % tm == 0
    tpb = per // tm
    row = pl.BlockSpec((tm, D_MODEL), lambda b, i: (b * tpb + i, 0))
    ng = len(POOL_WINDOWS)
    return pl.pallas_call(
        functools.partial(_pool_kernel, tm=tm, pos0=pos0),
        grid=(nbatch, tpb),
        in_specs=[row, _const_spec((1, D_MODEL)), _const_spec((HALO_P, D_MODEL)),
                  _const_spec((ng, POOL_GC, POOL_GC)), _const_spec((1, D_MODEL))],
        out_specs=[row, pl.BlockSpec((HALO_P, D_MODEL), lambda b, i: (b, 0))],
        out_shape=[jax.ShapeDtypeStruct((rows, D_MODEL), F32),
                   jax.ShapeDtypeStruct((nbatch * HALO_P, D_MODEL), F32)],
        scratch_shapes=[pltpu.VMEM((tm + HALO_P, D_MODEL), F32)],
        compiler_params=_params(),
        name="pool",
    )(x, gain.reshape(1, D_MODEL), hh, pw, ps.reshape(1, D_MODEL))


def _pools_kernel(x_ref, g_ref, sp_ref, pw_ref, ps_ref, o_ref, po_ref, *, sb, ns, nseq):
    base = pl.multiple_of(pl.program_id(0) * sb, sb)
    g = g_ref[...]
    xs = [x_ref[pl.ds(t * nseq + base, sb), :] for t in range(ns)]
    ext = [sp_ref[:, r, :] for r in range(POOL_HIST)] + [_rms(x, g) for x in xs]
    inv = [1.0 / win for win in POOL_WINDOWS]
    for t in range(ns):
        p_groups = []
        for gi, win in enumerate(POOL_WINDOWS):
            lanes = slice(gi * POOL_GC, (gi + 1) * POOL_GC)
            hg = ext[POOL_HIST + t][:, lanes]
            wsum = hg
            for d in range(1, win):
                wsum = wsum + ext[POOL_HIST + t - d][:, lanes]
            p_groups.append(wsum * inv[gi] - hg)
        o_ref[pl.ds(t * nseq + base, sb), :] = _group_linear(p_groups, pw_ref, ps_ref[...], xs[t])
    for r in range(POOL_HIST):
        po_ref[:, r, :] = ext[ns + r]


def _pools_call(x, gain, sp, pw, ps, *, sb, ns, tile):
    nseq = sp.shape[0]
    assert PAST_LEN + 1 >= max(POOL_WINDOWS) and nseq % sb == 0 and ns * nseq == TM
    state = pl.BlockSpec((sb, POOL_HIST, D_MODEL), lambda i: (i, 0, 0))
    ng = len(POOL_WINDOWS)
    return pl.pallas_call(
        functools.partial(_pools_kernel, sb=sb, ns=ns, nseq=nseq),
        grid=(nseq // sb,),
        in_specs=[pl.BlockSpec((TM, D_MODEL), lambda i: (tile, 0), pipeline_mode=pl.Buffered(1)),
                  _const_spec((1, D_MODEL)), state, _const_spec((ng, POOL_GC, POOL_GC)), _const_spec((1, D_MODEL))],
        out_specs=[pl.BlockSpec((TM, D_MODEL), lambda i: (0, 0)), state],
        out_shape=[jax.ShapeDtypeStruct((TM, D_MODEL), F32),
                   jax.ShapeDtypeStruct((nseq, POOL_HIST, D_MODEL), F32)],
        compiler_params=_params(),
        name="pool_sample",
    )(x, gain.reshape(1, D_MODEL), sp, pw, ps.reshape(1, D_MODEL))


def _rope_tables(pos):
    half = HEAD_DIM // 2
    inv = ROPE_THETA ** (-jnp.arange(half, dtype=F32) / half)
    ang = pos.astype(F32)[:, None] * inv[None, :]
    cos, sin = jnp.cos(ang), jnp.sin(ang)
    reps = LANES // HEAD_DIM
    cos_t = jnp.concatenate([cos, cos] * reps, axis=1)
    sin_t = jnp.concatenate([-sin, sin] * reps, axis=1)
    return cos_t, sin_t


def _head_perm():
    idx = []
    for c in range(GROUP):
        idx += list(range(c * HEAD_DIM, (c + 1) * HEAD_DIM))
        idx += list(range((GROUP + c) * HEAD_DIM, (GROUP + c + 1) * HEAD_DIM))
    return jnp.asarray(idx, dtype=jnp.int32)


def kernel(x_prompt, x_sample, cache_k, cache_v, state_conv, state_pool, meta_tokens, ln_gain, ffn_w_gate, ffn_w_up,
           ffn_w_down, mix_w_in, attn_sink, conv_w, mix_w_out, pool_w, pool_scale, final_gain):
    nb, seq, _ = x_prompt.shape
    nseq, ns, _ = x_sample.shape
    assert ln_gain.shape[0] == 2 and meta_tokens.shape[0] == N_META and cache_k.shape[2] == WINDOW
    assert seq % TM == 0 and ns * nseq == TM
    n_prompt = nb * seq // TM
    tpb = seq // TM

    perm = _head_perm()
    w_in = mix_w_in[0]
    w_in = jnp.concatenate([w_in[:, :ATT_W][:, perm], w_in[:, ATT_W:]], axis=1).astype(BF16)
    w_out = mix_w_out[0]
    w_out = jnp.concatenate([w_out[:ATT_W][perm], w_out[ATT_W:]], axis=0).astype(BF16)
    w_ffn = (ffn_w_gate.astype(BF16), ffn_w_up.astype(BF16), ffn_w_down.astype(BF16))
    pw = pool_w[0].astype(BF16)
    sink = attn_sink[0].reshape(N_Q_HEADS)
    cw = conv_w[0]

    pos_main = jnp.concatenate([N_META + jnp.arange(seq, dtype=jnp.int32),
                                jnp.repeat(PAST_LEN + jnp.arange(ns, dtype=jnp.int32), nseq)])
    cos_a, sin_a = _rope_tables(pos_main)
    cos_m, sin_m = _rope_tables(jnp.arange(META_TILE, dtype=jnp.int32) - META_PAD)

    xp = x_prompt.reshape(nb * seq, D_MODEL)
    xm = jnp.concatenate([jnp.zeros((META_PAD, D_MODEL), F32), meta_tokens.astype(F32)], axis=0)

    cfg = _StageCfg(n_main=n_prompt, sample_in="seq", inproj=True, ns=ns, nseq=nseq)
    outs = _stage_call(xm, xp, x_sample, ln_gain[0, 0], w_ffn, 0, 0, cfg=cfg,
                       proj=(ln_gain[0, 1], w_in, cos_m, sin_m, cos_a, sin_a, tpb), name="ffn_inproj")
    xm, xa = outs[0], outs[1]
    qm, km, vm, gbm, um = outs[2:7]
    qa, ka, va, gba, ua = outs[7:12]

    zk = jnp.zeros((BLOCK, KV_W), F32)
    zu = jnp.zeros((HALO_U, CONV_DIM), F32)
    xm, um_last = _mix0_call(sink, xm, qm, km, vm, gbm, um, zk, zk, zu, cw, w_out,
                             nbatch=1, tm=META_TILE, kmin_first=2 * BLOCK - N_META, rows=META_TILE)
    xp, up_last = _mix0_call(sink, xa, qa, ka, va, gba, ua, km, vm, um_last, cw, w_out,
                             nbatch=nb, tm=TM, kmin_first=BLOCK - N_META, rows=nb * seq)
    xs, k_s, v_s, conv_s = _mix0s_call(
        sink, xa, qa, ka, va, gba, ua,
        cache_k[0].reshape(nseq, WINDOW, KV_W), cache_v[0].reshape(nseq, WINDOW, KV_W), state_conv[0], cw, w_out,
        sb=SEQ_BLOCK, ns=ns, tile=n_prompt)

    cfg = _StageCfg(n_main=n_prompt, sample_in="rows", ns=ns, nseq=nseq)
    xm, xa = _stage_call(xm, xp, xs, ln_gain[0, 2], w_ffn, 0, 1, cfg=cfg, name="ffn_b0")

    cfg = _StageCfg(n_main=n_prompt + 1, ns=ns, nseq=nseq)
    xm, xa = _stage_call(xm, xa, None, ln_gain[1, 0], w_ffn, 1, 0, cfg=cfg, name="ffn_a1")

    zh = jnp.zeros((HALO_P, D_MODEL), F32)
    xm, hm_last = _pool_call(xm, ln_gain[1, 1], zh, pw, pool_scale[0], nbatch=1, tm=META_TILE, pos0=-META_PAD,
                             rows=META_TILE)
    xp, hp_last = _pool_call(xa, ln_gain[1, 1], hm_last, pw, pool_scale[0], nbatch=nb, tm=TM, pos0=N_META,
                             rows=nb * seq)
    xs, pool_s = _pools_call(xa, ln_gain[1, 1], state_pool[0], pw, pool_scale[0], sb=SEQ_BLOCK, ns=ns, tile=n_prompt)

    cfg = _StageCfg(n_main=n_prompt, sample_in="rows", sample_out=True, final=True, ns=ns, nseq=nseq)
    _, y_prompt, y_sample = _stage_call(xm, xp, xs, ln_gain[1, 2], w_ffn, 1, 1, cfg=cfg, final_gain=final_gain,
                                        name="ffn_final")

    kv_shape = (1, nb, WINDOW, N_KV_HEADS, HEAD_DIM)
    k_prompt = ka[:nb * seq].reshape(nb, seq, KV_W)[:, seq - WINDOW:].reshape(kv_shape)
    v_prompt = va[:nb * seq].reshape(nb, seq, KV_W)[:, seq - WINDOW:].reshape(kv_shape)
    conv_prompt = up_last.reshape(1, nb, HALO_U, CONV_DIM)[:, :, HALO_U - (CONV_W - 1):]
    pool_prompt = hp_last.reshape(1, nb, HALO_P, D_MODEL)[:, :, HALO_P - POOL_HIST:]
    skv_shape = (1, nseq, WINDOW, N_KV_HEADS, HEAD_DIM)
    return (y_prompt.reshape(nb, seq, D_MODEL), y_sample, k_prompt, v_prompt, conv_prompt, pool_prompt,
            k_s.reshape(skv_shape), v_s.reshape(skv_shape), conv_s[None], pool_s[None])
```

```python
import dataclasses
import functools

import jax
import jax.numpy as jnp
import numpy as np
from jax import lax
from jax.experimental import pallas as pl
from jax.experimental.pallas import tpu as pltpu

F32 = jnp.float32
BF16 = jnp.bfloat16

D_MODEL = 1024
N_META = 16
HEAD_DIM = 64
N_Q_HEADS = 8
N_KV_HEADS = 2
GROUP = N_Q_HEADS // N_KV_HEADS
WINDOW = 128
BLOCK = 128
ROPE_THETA = 10000.0
CONV_DIM = 512
CONV_W = 3
POOL_WINDOWS = (2, 4, 8, 16)
POOL_GC = D_MODEL // len(POOL_WINDOWS)
POOL_HIST = max(POOL_WINDOWS) - 1
D_FF = 2816
RMS_EPS = 1e-6
PAST_LEN = 8192
ATT_W = N_Q_HEADS * HEAD_DIM
KV_W = N_KV_HEADS * HEAD_DIM
IN_W = ATT_W + 2 * KV_W + 3 * CONV_DIM

V7X_VMEM_BYTES = 64 * 1024 * 1024
VMEM_LIMIT = V7X_VMEM_BYTES * 7 // 8
LANES = 128
META_TILE = BLOCK
META_PAD = META_TILE - N_META
HALO_U = 8
HALO_P = 16
TM = 512
SEQ_BLOCK = 32
NEW_PAD = 8


def _params():
    return pltpu.CompilerParams(vmem_limit_bytes=VMEM_LIMIT)


def _const_spec(shape):
    nd = len(shape)
    return pl.BlockSpec(shape, lambda *_: (0,) * nd, pipeline_mode=pl.Buffered(1))


def _rms(x, g):
    ms = jnp.mean(x * x, axis=-1, keepdims=True)
    return (x * lax.rsqrt(ms + RMS_EPS)) * g


def _ffn_body(x, g, wg_ref, wu_ref, wd_ref):
    h = _rms(x, g).astype(BF16)
    gate = jnp.dot(h, wg_ref[...], preferred_element_type=F32)
    up = jnp.dot(h, wu_ref[...], preferred_element_type=F32)
    act = ((gate * (1.0 / (1.0 + jnp.exp(-gate)))) * up).astype(BF16)
    return x + 0.5 * jnp.dot(act, wd_ref[...], preferred_element_type=F32)


def _rope(x, cos, sin_signed):
    w = x.shape[1]
    lane = lax.broadcasted_iota(jnp.int32, x.shape, 1)
    first_half = (lane % HEAD_DIM) < (HEAD_DIM // 2)
    partner = jnp.where(first_half, pltpu.roll(x, w - HEAD_DIM // 2, 1), pltpu.roll(x, HEAD_DIM // 2, 1))
    reps = w // LANES
    if reps > 1:
        cos = jnp.concatenate([cos] * reps, axis=1)
        sin_signed = jnp.concatenate([sin_signed] * reps, axis=1)
    return x * cos + partner * sin_signed


def _inproj_body(x, g, w_ref, cos, sin_signed):
    h = _rms(x, g).astype(BF16)
    p = jnp.dot(h, w_ref[...], preferred_element_type=F32)
    o = 0
    q = _rope(p[:, o:o + ATT_W], cos, sin_signed) * (HEAD_DIM ** -0.5)
    o += ATT_W
    k = _rope(p[:, o:o + KV_W], cos, sin_signed)
    o += KV_W
    v = p[:, o:o + KV_W]
    o += KV_W
    gb = p[:, o:o + CONV_DIM]
    o += CONV_DIM
    gc = p[:, o:o + CONV_DIM]
    o += CONV_DIM
    hc = p[:, o:o + CONV_DIM]
    return q, k, v, gb, gc * hc


PROJ_OUT = ((ATT_W, BF16), (KV_W, F32), (KV_W, F32), (CONV_DIM, F32), (CONV_DIM, F32))


@dataclasses.dataclass(frozen=True)
class _StageCfg:
    n_main: int
    sample_in: str = ""
    sample_out: bool = False
    inproj: bool = False
    final: bool = False
    ns: int = 4
    nseq: int = 128


def _stage_kernel(*refs, cfg):
    it = iter(refs)
    xm_ref, xa_ref = next(it), next(it)
    xs_ref = next(it) if cfg.sample_in else None
    g_ref, wg_ref, wu_ref, wd_ref = next(it), next(it), next(it), next(it)
    if cfg.inproj:
        g1_ref, win_ref, cosm_ref, sinm_ref, cosa_ref, sina_ref = (next(it) for _ in range(6))
    fg_ref = next(it) if cfg.final else None
    om_ref, oa_ref = next(it), next(it)
    os_ref = next(it) if cfg.sample_out else None
    if cfg.inproj:
        pm_refs = [next(it) for _ in PROJ_OUT]
        pa_refs = [next(it) for _ in PROJ_OUT]
    x_scr = next(it) if (cfg.sample_in and not cfg.sample_out) else None

    i = pl.program_id(0)
    first_sample = cfg.n_main + 1 if cfg.sample_in else cfg.n_main

    def run(x, store, tab_refs, p_refs):
        y = _ffn_body(x, g_ref[...], wg_ref, wu_ref, wd_ref)
        if cfg.final:
            y = _rms(y, fg_ref[...])
        store(y)
        if cfg.inproj:
            outs = _inproj_body(y, g1_ref[...], win_ref, tab_refs[0][...], tab_refs[1][...])
            for r, val in zip(p_refs, outs):
                r[...] = val.astype(r.dtype)

    def store_to(o_ref):
        def store(y):
            o_ref[...] = y
        return store

    def store_sample(y):
        for t in range(cfg.ns):
            os_ref[:, t, :] = y[t * cfg.nseq:(t + 1) * cfg.nseq]

    @pl.when(i == 0)
    def _():
        run(xm_ref[...], store_to(om_ref), (cosm_ref, sinm_ref) if cfg.inproj else None,
            pm_refs if cfg.inproj else None)

    if cfg.sample_out:
        @pl.when((i >= 1) & (i < first_sample))
        def _():
            run(xa_ref[...], store_to(oa_ref), None, None)

        @pl.when(i == first_sample)
        def _():
            run(xs_ref[...], store_sample, None, None)
        return

    if cfg.sample_in:
        @pl.when((i >= 1) & (i < first_sample))
        def _():
            x_scr[...] = xa_ref[...]

        @pl.when(i == first_sample)
        def _():
            if cfg.sample_in == "seq":
                for t in range(cfg.ns):
                    x_scr[t * cfg.nseq:(t + 1) * cfg.nseq, :] = xs_ref[:, t, :]
            else:
                x_scr[...] = xs_ref[...]

    @pl.when(i >= 1)
    def _():
        x = x_scr[...] if cfg.sample_in else xa_ref[...]
        run(x, store_to(oa_ref), (cosa_ref, sina_ref) if cfg.inproj else None, pa_refs if cfg.inproj else None)


def _stage_call(xm, xa, xs, gain, w_ffn, layer, which, *, cfg, proj=None, final_gain=None, name="stage"):
    wg, wu, wd = w_ffn
    n_in = cfg.n_main
    n_out = cfg.n_main if cfg.sample_out else (cfg.n_main + 1 if cfg.sample_in else cfg.n_main)
    steps = 1 + (n_in + 1 if cfg.sample_in else n_in)
    assert xa.shape[0] == n_in * TM and xm.shape[0] == META_TILE

    def main_spec(w, n):
        return pl.BlockSpec((TM, w), lambda i: (jnp.clip(i - 1, 0, n - 1), 0))

    meta_spec = lambda w: _const_spec((META_TILE, w))
    wsel = lambda shape: pl.BlockSpec((None, None) + shape, lambda i: (layer, which, 0, 0),
                                      pipeline_mode=pl.Buffered(1))
    in_specs = [meta_spec(D_MODEL), main_spec(D_MODEL, n_in)]
    args = [xm, xa]
    if cfg.sample_in:
        in_specs.append(_const_spec(xs.shape))
        args.append(xs)
    in_specs += [_const_spec((1, D_MODEL)), wsel((D_MODEL, D_FF)), wsel((D_MODEL, D_FF)), wsel((D_FF, D_MODEL))]
    args += [gain.reshape(1, D_MODEL), wg, wu, wd]
    if cfg.inproj:
        g1, w_in, cos_m, sin_m, cos_a, sin_a, tpb = proj
        n_tab = cos_a.shape[0] // TM
        tab = pl.BlockSpec((TM, LANES), lambda i: (jnp.where(i - 1 < cfg.n_main, jnp.maximum(i - 1, 0) % tpb, n_tab - 1), 0))
        in_specs += [_const_spec((1, D_MODEL)), _const_spec((D_MODEL, IN_W)), meta_spec(LANES), meta_spec(LANES), tab, tab]
        args += [g1.reshape(1, D_MODEL), w_in, cos_m, sin_m, cos_a, sin_a]
    if cfg.final:
        in_specs.append(_const_spec((1, D_MODEL)))
        args.append(final_gain.reshape(1, D_MODEL))

    out_specs = [meta_spec(D_MODEL), main_spec(D_MODEL, n_out)]
    out_shape = [jax.ShapeDtypeStruct((META_TILE, D_MODEL), F32), jax.ShapeDtypeStruct((n_out * TM, D_MODEL), F32)]
    if cfg.sample_out:
        out_specs.append(_const_spec((cfg.nseq, cfg.ns, D_MODEL)))
        out_shape.append(jax.ShapeDtypeStruct((cfg.nseq, cfg.ns, D_MODEL), F32))
    if cfg.inproj:
        out_specs += [meta_spec(w) for w, _ in PROJ_OUT] + [main_spec(w, n_out) for w, _ in PROJ_OUT]
        out_shape += [jax.ShapeDtypeStruct((META_TILE, w), dt) for w, dt in PROJ_OUT]
        out_shape += [jax.ShapeDtypeStruct((n_out * TM, w), dt) for w, dt in PROJ_OUT]
    scratch = []
    if cfg.sample_in and not cfg.sample_out:
        scratch.append(pltpu.VMEM((TM, D_MODEL), F32))
    return pl.pallas_call(
        functools.partial(_stage_kernel, cfg=cfg),
        grid=(steps,),
        in_specs=in_specs,
        out_specs=out_specs,
        out_shape=out_shape,
        scratch_shapes=scratch,
        compiler_params=_params(),
        name=name,
    )(*args)


def _softmax_pv(s, bias, sinks, vv):
    r, kk = bias.shape
    lane = lax.broadcasted_iota(jnp.int32, (r, LANES), 1)
    p_rows, inv = [], []
    for c in range(GROUP):
        p_halves, inv_halves = [], []
        for h in range(N_KV_HEADS):
            sb = s[c * r:(c + 1) * r, h * kk:(h + 1) * kk] + bias
            snk = sinks[h][c]
            m = jnp.maximum(jnp.max(sb, axis=1, keepdims=True), snk)
            p = jnp.exp(sb - m)
            l = jnp.sum(p, axis=1, keepdims=True) + jnp.exp(snk - m)
            p_halves.append(p.astype(BF16))
            inv_halves.append(1.0 / l)
        p_rows.append(jnp.concatenate(p_halves, axis=1))
        inv.append(jnp.where(lane < HEAD_DIM, inv_halves[0], inv_halves[1]))
    o = jnp.dot(jnp.concatenate(p_rows, axis=0), vv, preferred_element_type=F32)
    return [o[c * r:(c + 1) * r] * inv[c] for c in range(GROUP)]


def _mix0_kernel(sink_ref, x_ref, q_ref, k_ref, v_ref, gb_ref, u_ref, hk_ref, hv_ref, hu_ref, cw_ref, wout_ref,
                 o_ref, ulast_ref, klast_ref, vlast_ref, k0h, k1h, v0h, v1h, ubuf, *, tm, kmin_first):
    i = pl.program_id(1)
    lane_h = lax.broadcasted_iota(jnp.int32, (BLOCK, LANES), 1) < HEAD_DIM

    @pl.when(i == 0)
    def _():
        hk = hk_ref[...]
        hv = hv_ref[...]
        k0h[...] = jnp.where(lane_h, hk, 0.0).astype(BF16)
        k1h[...] = jnp.where(lane_h, 0.0, hk).astype(BF16)
        v0h[...] = jnp.where(lane_h, hv, 0.0).astype(BF16)
        v1h[...] = jnp.where(lane_h, 0.0, hv).astype(BF16)
        ubuf[0:HALO_U, :] = hu_ref[...]

    lane_t = lax.broadcasted_iota(jnp.int32, (tm, LANES), 1) < HEAD_DIM
    kf = k_ref[...]
    vf = v_ref[...]
    k0 = jnp.concatenate([k0h[...], jnp.where(lane_t, kf, 0.0).astype(BF16)], axis=0)
    k1 = jnp.concatenate([k1h[...], jnp.where(lane_t, 0.0, kf).astype(BF16)], axis=0)
    v0 = jnp.concatenate([v0h[...], jnp.where(lane_t, vf, 0.0).astype(BF16)], axis=0)
    v1 = jnp.concatenate([v1h[...], jnp.where(lane_t, 0.0, vf).astype(BF16)], axis=0)
    k0h[...] = k0[tm:tm + BLOCK]
    k1h[...] = k1[tm:tm + BLOCK]
    v0h[...] = v0[tm:tm + BLOCK]
    v1h[...] = v1[tm:tm + BLOCK]
    klast_ref[...] = kf[tm - BLOCK:tm]
    vlast_ref[...] = vf[tm - BLOCK:tm]

    sinks = [[sink_ref[h * GROUP + c] for c in range(GROUP)] for h in range(N_KV_HEADS)]
    row = lax.broadcasted_iota(jnp.int32, (BLOCK, 2 * BLOCK), 0)
    col = lax.broadcasted_iota(jnp.int32, (BLOCK, 2 * BLOCK), 1)
    band = (col >= row) & (col <= row + WINDOW)
    bias_rest = jnp.where(band, 0.0, -jnp.inf)
    kmin = jnp.where(i == 0, kmin_first, 0)
    bias_first = jnp.where(band & (col >= kmin), 0.0, -jnp.inf)

    q = q_ref[...]
    att_blocks = []
    for j in range(tm // BLOCK):
        ks = slice(j * BLOCK, (j + 2) * BLOCK)
        kk = jnp.concatenate([k0[ks], k1[ks]], axis=0)
        vv = jnp.concatenate([v0[ks], v1[ks]], axis=0)
        qs = jnp.concatenate([q[j * BLOCK:(j + 1) * BLOCK, c * LANES:(c + 1) * LANES] for c in range(GROUP)], axis=0)
        s = lax.dot_general(qs, kk, (((1,), (1,)), ((), ())), preferred_element_type=F32)
        o = _softmax_pv(s, bias_first if j == 0 else bias_rest, sinks, vv)
        att_blocks.append(jnp.concatenate(o, axis=1))
    att = jnp.concatenate(att_blocks, axis=0) if len(att_blocks) > 1 else att_blocks[0]

    u = u_ref[...]
    ubuf[HALO_U:HALO_U + tm, :] = u
    cw = cw_ref[...]
    conv = cw[0:1] * ubuf[HALO_U - 2:HALO_U - 2 + tm, :] + cw[1:2] * ubuf[HALO_U - 1:HALO_U - 1 + tm, :] + cw[2:3] * u
    u_tail = u[tm - HALO_U:tm]
    ubuf[0:HALO_U, :] = u_tail
    ulast_ref[...] = u_tail

    mix_in = jnp.concatenate([att, gb_ref[...] * conv], axis=1).astype(BF16)
    o_ref[...] = x_ref[...] + jnp.dot(mix_in, wout_ref[...], preferred_element_type=F32)


def _mix0_call(sink, x, q, k, v, gb, u, hk, hv, hu, cw, wout, *, nbatch, tm, kmin_first, rows):
    per = rows // nbatch
    assert per % tm == 0 and tm % BLOCK == 0
    tpb = per // tm
    row = lambda w: pl.BlockSpec((tm, w), lambda b, i: (b * tpb + i, 0))
    return pl.pallas_call(
        functools.partial(_mix0_kernel, tm=tm, kmin_first=kmin_first),
        grid=(nbatch, tpb),
        in_specs=[pl.BlockSpec(memory_space=pltpu.SMEM),
                  row(D_MODEL), row(ATT_W), row(KV_W), row(KV_W), row(CONV_DIM), row(CONV_DIM),
                  _const_spec((BLOCK, KV_W)), _const_spec((BLOCK, KV_W)), _const_spec((HALO_U, CONV_DIM)),
                  _const_spec((CONV_W, CONV_DIM)), _const_spec((D_MODEL, D_MODEL))],
        out_specs=[row(D_MODEL), pl.BlockSpec((HALO_U, CONV_DIM), lambda b, i: (b, 0)),
                   pl.BlockSpec((BLOCK, KV_W), lambda b, i: (b, 0)), pl.BlockSpec((BLOCK, KV_W), lambda b, i: (b, 0))],
        out_shape=[jax.ShapeDtypeStruct((rows, D_MODEL), F32),
                   jax.ShapeDtypeStruct((nbatch * HALO_U, CONV_DIM), F32),
                   jax.ShapeDtypeStruct((nbatch * BLOCK, KV_W), F32),
                   jax.ShapeDtypeStruct((nbatch * BLOCK, KV_W), F32)],
        scratch_shapes=[pltpu.VMEM((BLOCK, KV_W), BF16)] * 4 + [pltpu.VMEM((tm + HALO_U, CONV_DIM), F32)],
        compiler_params=_params(),
        name="mix0",
    )(sink, x, q, k, v, gb, u, hk, hv, hu, cw, wout)


def _mix0s_kernel(sink_ref, x_ref, q_ref, k_ref, v_ref, gb_ref, u_ref, ck_ref, cv_ref, sc_ref, cw_ref, wout_ref,
                  o_ref, ko_ref, vo_ref, co_ref, qs_scr, os_scr, knew, vnew, *, sb, ns, nseq):
    ext = WINDOW + NEW_PAD
    base = pl.multiple_of(pl.program_id(0) * sb, sb)
    rows_t = lambda ref, t: ref[pl.ds(t * nseq + base, sb), :]
    lane_h = lax.broadcasted_iota(jnp.int32, (sb, LANES), 1) < HEAD_DIM
    for t in range(ns):
        qt = rows_t(q_ref, t).astype(F32)
        for c in range(GROUP):
            chunk = qt[:, c * LANES:(c + 1) * LANES]
            qs_scr[(0 * GROUP + c) * ns + t] = jnp.where(lane_h, chunk, 0.0)
            qs_scr[(1 * GROUP + c) * ns + t] = jnp.where(lane_h, 0.0, chunk)
        knew[t] = rows_t(k_ref, t)
        vnew[t] = rows_t(v_ref, t)
    nrow = N_KV_HEADS * GROUP * ns
    rid = lax.broadcasted_iota(jnp.int32, (nrow, ext), 0)
    e = lax.broadcasted_iota(jnp.int32, (nrow, ext), 1)
    t_of = rid % ns
    bias = jnp.where((e >= t_of) & (e <= t_of + WINDOW), 0.0, -jnp.inf)
    rid1 = lax.broadcasted_iota(jnp.int32, (nrow, 1), 0)
    snk = jnp.zeros((nrow, 1), F32)
    for hc in range(N_KV_HEADS * GROUP):
        snk = jnp.where(rid1 // ns == hc, sink_ref[hc], snk)
    zrows = jnp.zeros((NEW_PAD - ns, sb, LANES), F32)
    knew[ns:NEW_PAD] = zrows
    vnew[ns:NEW_PAD] = zrows
    for b in range(sb):
        kb = jnp.concatenate([ck_ref[b], knew[:, b, :]], axis=0)
        vb = jnp.concatenate([cv_ref[b], vnew[:, b, :]], axis=0)
        ko_ref[b] = kb[ns:ns + WINDOW]
        vo_ref[b] = vb[ns:ns + WINDOW]
        qb = qs_scr[:, b, :].astype(BF16)
        s = lax.dot_general(qb, kb.astype(BF16), (((1,), (1,)), ((), ())), preferred_element_type=F32) + bias
        m = jnp.maximum(jnp.max(s, axis=1, keepdims=True), snk)
        p = jnp.exp(s - m)
        l = jnp.sum(p, axis=1, keepdims=True) + jnp.exp(snk - m)
        o = jnp.dot(p.astype(BF16), vb.astype(BF16), preferred_element_type=F32)
        os_scr[:, b, :] = o * (1.0 / l)
    cw = cw_ref[...]
    ue = [sc_ref[r] for r in range(CONV_W - 1)] + [rows_t(u_ref, t) for t in range(ns)]
    rows = []
    for t in range(ns):
        att = jnp.concatenate(
            [jnp.where(lane_h, os_scr[(0 * GROUP + c) * ns + t], os_scr[(1 * GROUP + c) * ns + t]) for c in range(GROUP)],
            axis=1)
        conv = cw[0:1] * ue[t] + cw[1:2] * ue[t + 1] + cw[2:3] * ue[t + 2]
        rows.append(jnp.concatenate([att, rows_t(gb_ref, t) * conv], axis=1))
    mix_in = jnp.concatenate(rows, axis=0).astype(BF16)
    y = jnp.dot(mix_in, wout_ref[...], preferred_element_type=F32)
    for t in range(ns):
        o_ref[pl.ds(t * nseq + base, sb), :] = rows_t(x_ref, t) + y[t * sb:(t + 1) * sb]
    for r in range(CONV_W - 1):
        co_ref[r] = ue[ns + r]


def _mix0s_call(sink, x, q, k, v, gb, u, ck, cv, sc, cw, wout, *, sb, ns, tile):
    nseq = ck.shape[0]
    assert nseq % sb == 0 and ns * nseq == TM
    res = lambda w: pl.BlockSpec((TM, w), lambda i: (tile, 0), pipeline_mode=pl.Buffered(1))
    cache = pl.BlockSpec((sb, WINDOW, KV_W), lambda i: (i, 0, 0))
    cstate = pl.BlockSpec((CONV_W - 1, sb, CONV_DIM), lambda i: (0, i, 0))
    nrow = N_KV_HEADS * GROUP * ns
    return pl.pallas_call(
        functools.partial(_mix0s_kernel, sb=sb, ns=ns, nseq=nseq),
        grid=(nseq // sb,),
        in_specs=[pl.BlockSpec(memory_space=pltpu.SMEM),
                  res(D_MODEL), res(ATT_W), res(KV_W), res(KV_W), res(CONV_DIM), res(CONV_DIM),
                  cache, cache, cstate, _const_spec((CONV_W, CONV_DIM)), _const_spec((D_MODEL, D_MODEL))],
        out_specs=[pl.BlockSpec((TM, D_MODEL), lambda i: (0, 0)), cache, cache, cstate],
        out_shape=[jax.ShapeDtypeStruct((TM, D_MODEL), F32),
                   jax.ShapeDtypeStruct((nseq, WINDOW, KV_W), F32),
                   jax.ShapeDtypeStruct((nseq, WINDOW, KV_W), F32),
                   jax.ShapeDtypeStruct((CONV_W - 1, nseq, CONV_DIM), F32)],
        scratch_shapes=[pltpu.VMEM((nrow, sb, LANES), F32), pltpu.VMEM((nrow, sb, LANES), F32),
                        pltpu.VMEM((NEW_PAD, sb, KV_W), F32), pltpu.VMEM((NEW_PAD, sb, KV_W), F32)],
        compiler_params=_params(),
        name="mix0_sample",
    )(sink, x, q, k, v, gb, u, ck, cv, sc, cw, wout)


def _group_linear(p_groups, pw_ref, ps, x):
    z = jnp.concatenate(
        [jnp.dot(p.astype(BF16), pw_ref[g], preferred_element_type=F32) for g, p in enumerate(p_groups)], axis=1)
    return x + z * ps


assert POOL_WINDOWS == tuple(2 ** (g + 1) for g in range(len(POOL_WINDOWS)))
POOL_LO = 8
POOL_BASE = POOL_LO + HALO_P


def _window_sums(hbuf, lvl_bufs, tm):
    n = POOL_BASE + tm
    cur = hbuf[POOL_LO:n, :] + hbuf[POOL_LO - 1:n - 1, :]
    sums = [cur[HALO_P:, 0:POOL_GC]]
    w = 2
    for buf in lvl_bufs:
        rest = cur[:, POOL_GC:]
        buf[POOL_LO:n, :] = rest
        cur = rest + buf[POOL_LO - w:n - w, :]
        sums.append(cur[HALO_P:, 0:POOL_GC])
        w *= 2
    return sums


def _pool_kernel(x_ref, g_ref, hh_ref, pw_ref, ps_ref, o_ref, hlast_ref, hbuf, *lvl_bufs, tm, pos0):
    i = pl.program_id(1)

    @pl.when(i == 0)
    def _():
        hbuf[0:POOL_LO, :] = jnp.zeros((POOL_LO, D_MODEL), F32)
        for buf in lvl_bufs:
            buf[0:POOL_LO, :] = jnp.zeros((POOL_LO, buf.shape[1]), F32)
        hbuf[POOL_LO:POOL_BASE, :] = hh_ref[...]

    x = x_ref[...]
    h = _rms(x, g_ref[...])
    hbuf[POOL_BASE:POOL_BASE + tm, :] = h
    pos = pos0 + i * tm + lax.broadcasted_iota(jnp.int32, (tm, 1), 0)
    p_groups = []
    for g, (win, wsum) in enumerate(zip(POOL_WINDOWS, _window_sums(hbuf, lvl_bufs, tm))):
        cnt = jnp.clip(pos + 1, 1, win).astype(F32)
        p_groups.append(wsum * (1.0 / cnt) - h[:, g * POOL_GC:(g + 1) * POOL_GC])
    o_ref[...] = _group_linear(p_groups, pw_ref, ps_ref[...], x)
    tail = h[tm - HALO_P:tm]
    hbuf[POOL_LO:POOL_BASE, :] = tail
    hlast_ref[...] = tail


def _pool_call(x, gain, hh, pw, ps, *, nbatch, tm, pos0, rows):
    per = rows // nbatch
    assert per % tm == 0
    tpb = per // tm
    row = pl.BlockSpec((tm, D_MODEL), lambda b, i: (b * tpb + i, 0))
    ng = len(POOL_WINDOWS)
    return pl.pallas_call(
        functools.partial(_pool_kernel, tm=tm, pos0=pos0),
        grid=(nbatch, tpb),
        in_specs=[row, _const_spec((1, D_MODEL)), _const_spec((HALO_P, D_MODEL)),
                  _const_spec((ng, POOL_GC, POOL_GC)), _const_spec((1, D_MODEL))],
        out_specs=[row, pl.BlockSpec((HALO_P, D_MODEL), lambda b, i: (b, 0))],
        out_shape=[jax.ShapeDtypeStruct((rows, D_MODEL), F32),
                   jax.ShapeDtypeStruct((nbatch * HALO_P, D_MODEL), F32)],
        scratch_shapes=[pltpu.VMEM((POOL_BASE + tm, D_MODEL - g * POOL_GC), F32) for g in range(ng)],
        compiler_params=_params(),
        name="pool",
    )(x, gain.reshape(1, D_MODEL), hh, pw, ps.reshape(1, D_MODEL))


def _pools_kernel(x_ref, g_ref, sp_ref, pw_ref, ps_ref, o_ref, po_ref, *, sb, ns, nseq):
    base = pl.multiple_of(pl.program_id(0) * sb, sb)
    g = g_ref[...]
    xs = [x_ref[pl.ds(t * nseq + base, sb), :] for t in range(ns)]
    ext = [sp_ref[r] for r in range(POOL_HIST)] + [_rms(x, g) for x in xs]
    inv = [1.0 / win for win in POOL_WINDOWS]
    for t in range(ns):
        p_groups = []
        for gi, win in enumerate(POOL_WINDOWS):
            lanes = slice(gi * POOL_GC, (gi + 1) * POOL_GC)
            hg = ext[POOL_HIST + t][:, lanes]
            wsum = hg
            for d in range(1, win):
                wsum = wsum + ext[POOL_HIST + t - d][:, lanes]
            p_groups.append(wsum * inv[gi] - hg)
        o_ref[pl.ds(t * nseq + base, sb), :] = _group_linear(p_groups, pw_ref, ps_ref[...], xs[t])
    for r in range(POOL_HIST):
        po_ref[r] = ext[ns + r]


def _pools_call(x, gain, sp, pw, ps, *, sb, ns, tile):
    nseq = sp.shape[1]
    assert PAST_LEN + 1 >= max(POOL_WINDOWS) and nseq % sb == 0 and ns * nseq == TM
    state = pl.BlockSpec((POOL_HIST, sb, D_MODEL), lambda i: (0, i, 0))
    ng = len(POOL_WINDOWS)
    return pl.pallas_call(
        functools.partial(_pools_kernel, sb=sb, ns=ns, nseq=nseq),
        grid=(nseq // sb,),
        in_specs=[pl.BlockSpec((TM, D_MODEL), lambda i: (tile, 0), pipeline_mode=pl.Buffered(1)),
                  _const_spec((1, D_MODEL)), state, _const_spec((ng, POOL_GC, POOL_GC)), _const_spec((1, D_MODEL))],
        out_specs=[pl.BlockSpec((TM, D_MODEL), lambda i: (0, 0)), state],
        out_shape=[jax.ShapeDtypeStruct((TM, D_MODEL), F32),
                   jax.ShapeDtypeStruct((POOL_HIST, nseq, D_MODEL), F32)],
        compiler_params=_params(),
        name="pool_sample",
    )(x, gain.reshape(1, D_MODEL), sp, pw, ps.reshape(1, D_MODEL))


def _rope_tables(pos):
    half = HEAD_DIM // 2
    inv = ROPE_THETA ** (-np.arange(half, dtype=np.float64) / half)
    ang = np.asarray(pos, dtype=np.float64)[:, None] * inv[None, :]
    cos, sin = np.cos(ang), np.sin(ang)
    reps = LANES // HEAD_DIM
    cos_t = np.concatenate([cos, cos] * reps, axis=1).astype(np.float32)
    sin_t = np.concatenate([-sin, sin] * reps, axis=1).astype(np.float32)
    return jnp.asarray(cos_t), jnp.asarray(sin_t)


def _pair_heads(w, axis):
    shape = w.shape
    split = shape[:axis] + (N_KV_HEADS, GROUP, HEAD_DIM) + shape[axis + 1:]
    return jnp.swapaxes(w.reshape(split), axis, axis + 1).reshape(shape)


def kernel(x_prompt, x_sample, cache_k, cache_v, state_conv, state_pool, meta_tokens, ln_gain, ffn_w_gate, ffn_w_up,
           ffn_w_down, mix_w_in, attn_sink, conv_w, mix_w_out, pool_w, pool_scale, final_gain):
    nb, seq, _ = x_prompt.shape
    nseq, ns, _ = x_sample.shape
    assert ln_gain.shape[0] == 2 and meta_tokens.shape[0] == N_META and cache_k.shape[2] == WINDOW
    assert seq % TM == 0 and ns * nseq == TM
    n_prompt = nb * seq // TM
    tpb = seq // TM

    w_in = mix_w_in[0].astype(BF16)
    w_in = jnp.concatenate([_pair_heads(w_in[:, :ATT_W], 1), w_in[:, ATT_W:]], axis=1)
    w_out = mix_w_out[0].astype(BF16)
    w_out = jnp.concatenate([_pair_heads(w_out[:ATT_W], 0), w_out[ATT_W:]], axis=0)
    w_ffn = (ffn_w_gate.astype(BF16), ffn_w_up.astype(BF16), ffn_w_down.astype(BF16))
    pw = pool_w[0].astype(BF16)
    sink = attn_sink[0].reshape(N_Q_HEADS)
    cw = conv_w[0]

    cos_a, sin_a = _rope_tables(np.concatenate([N_META + np.arange(seq), np.repeat(PAST_LEN + np.arange(ns), nseq)]))
    cos_m, sin_m = _rope_tables(np.arange(META_TILE) - META_PAD)

    xp = x_prompt.reshape(nb * seq, D_MODEL)
    xm = jnp.concatenate([jnp.zeros((META_PAD, D_MODEL), F32), meta_tokens.astype(F32)], axis=0)

    cfg = _StageCfg(n_main=n_prompt, sample_in="seq", inproj=True, ns=ns, nseq=nseq)
    outs = _stage_call(xm, xp, x_sample, ln_gain[0, 0], w_ffn, 0, 0, cfg=cfg,
                       proj=(ln_gain[0, 1], w_in, cos_m, sin_m, cos_a, sin_a, tpb), name="ffn_inproj")
    xm, xa = outs[0], outs[1]
    qm, km, vm, gbm, um = outs[2:7]
    qa, ka, va, gba, ua = outs[7:12]

    zk = jnp.zeros((BLOCK, KV_W), F32)
    zu = jnp.zeros((HALO_U, CONV_DIM), F32)
    xm, um_last, _, _ = _mix0_call(sink, xm, qm, km, vm, gbm, um, zk, zk, zu, cw, w_out,
                                   nbatch=1, tm=META_TILE, kmin_first=2 * BLOCK - N_META, rows=META_TILE)
    xp, up_last, kp_last, vp_last = _mix0_call(sink, xa, qa, ka, va, gba, ua, km, vm, um_last, cw, w_out,
                                               nbatch=nb, tm=TM, kmin_first=BLOCK - N_META, rows=nb * seq)
    xs, k_s, v_s, conv_s = _mix0s_call(
        sink, xa, qa, ka, va, gba, ua,
        cache_k[0].reshape(nseq, WINDOW, KV_W), cache_v[0].reshape(nseq, WINDOW, KV_W),
        jnp.swapaxes(state_conv[0], 0, 1), cw, w_out, sb=SEQ_BLOCK, ns=ns, tile=n_prompt)

    cfg = _StageCfg(n_main=n_prompt, sample_in="rows", ns=ns, nseq=nseq)
    xm, xa = _stage_call(xm, xp, xs, ln_gain[0, 2], w_ffn, 0, 1, cfg=cfg, name="ffn_b0")

    cfg = _StageCfg(n_main=n_prompt + 1, ns=ns, nseq=nseq)
    xm, xa = _stage_call(xm, xa, None, ln_gain[1, 0], w_ffn, 1, 0, cfg=cfg, name="ffn_a1")

    zh = jnp.zeros((HALO_P, D_MODEL), F32)
    xm, hm_last = _pool_call(xm, ln_gain[1, 1], zh, pw, pool_scale[0], nbatch=1, tm=META_TILE, pos0=-META_PAD,
                             rows=META_TILE)
    xp, hp_last = _pool_call(xa, ln_gain[1, 1], hm_last, pw, pool_scale[0], nbatch=nb, tm=TM, pos0=N_META,
                             rows=nb * seq)
    xs, pool_s = _pools_call(xa, ln_gain[1, 1], jnp.swapaxes(state_pool[0], 0, 1), pw, pool_scale[0],
                             sb=SEQ_BLOCK, ns=ns, tile=n_prompt)

    cfg = _StageCfg(n_main=n_prompt, sample_in="rows", sample_out=True, final=True, ns=ns, nseq=nseq)
    _, y_prompt, y_sample = _stage_call(xm, xp, xs, ln_gain[1, 2], w_ffn, 1, 1, cfg=cfg, final_gain=final_gain,
                                        name="ffn_final")

    kv_shape = (1, nb, WINDOW, N_KV_HEADS, HEAD_DIM)
    k_prompt = kp_last.reshape(kv_shape)
    v_prompt = vp_last.reshape(kv_shape)
    conv_prompt = up_last.reshape(1, nb, HALO_U, CONV_DIM)[:, :, HALO_U - (CONV_W - 1):]
    pool_prompt = hp_last.reshape(1, nb, HALO_P, D_MODEL)[:, :, HALO_P - POOL_HIST:]
    skv_shape = (1, nseq, WINDOW, N_KV_HEADS, HEAD_DIM)
    return (y_prompt.reshape(nb, seq, D_MODEL), y_sample, k_prompt, v_prompt, conv_prompt, pool_prompt,
            k_s.reshape(skv_shape), v_s.reshape(skv_shape),
            jnp.swapaxes(conv_s, 0, 1)[None], jnp.swapaxes(pool_s, 0, 1)[None])
```

```python
import dataclasses
import functools

import jax
import jax.numpy as jnp
import numpy as np
from jax import lax
from jax.experimental import pallas as pl
from jax.experimental.pallas import tpu as pltpu

F32 = jnp.float32
BF16 = jnp.bfloat16

D_MODEL = 1024
N_META = 16
HEAD_DIM = 64
N_Q_HEADS = 8
N_KV_HEADS = 2
GROUP = N_Q_HEADS // N_KV_HEADS
WINDOW = 128
BLOCK = 128
ROPE_THETA = 10000.0
CONV_DIM = 512
CONV_W = 3
POOL_WINDOWS = (2, 4, 8, 16)
POOL_GC = D_MODEL // len(POOL_WINDOWS)
POOL_HIST = max(POOL_WINDOWS) - 1
D_FF = 2816
RMS_EPS = 1e-6
PAST_LEN = 8192
ATT_W = N_Q_HEADS * HEAD_DIM
KV_W = N_KV_HEADS * HEAD_DIM
IN_W = ATT_W + 2 * KV_W + 3 * CONV_DIM

V7X_VMEM_BYTES = 64 * 1024 * 1024
VMEM_LIMIT = V7X_VMEM_BYTES * 7 // 8
LANES = 128
META_TILE = BLOCK
META_PAD = META_TILE - N_META
HALO_U = 8
HALO_P = 16
TM = 512
SEQ_BLOCK = 32
NEW_PAD = 8


def _params():
    return pltpu.CompilerParams(vmem_limit_bytes=VMEM_LIMIT)


def _const_spec(shape):
    nd = len(shape)
    return pl.BlockSpec(shape, lambda *_: (0,) * nd, pipeline_mode=pl.Buffered(1))


def _rms(x, g):
    ms = jnp.mean(x * x, axis=-1, keepdims=True)
    return (x * lax.rsqrt(ms + RMS_EPS)) * g


def _ffn_body(x, g, wg_ref, wu_ref, wd_ref):
    h = _rms(x, g).astype(BF16)
    gate = jnp.dot(h, wg_ref[...], preferred_element_type=F32)
    up = jnp.dot(h, wu_ref[...], preferred_element_type=F32)
    act = ((gate * (1.0 / (1.0 + jnp.exp(-gate)))) * up).astype(BF16)
    return x + 0.5 * jnp.dot(act, wd_ref[...], preferred_element_type=F32)


def _rope(x, cos, sin_signed):
    w = x.shape[1]
    lane = lax.broadcasted_iota(jnp.int32, x.shape, 1)
    first_half = (lane % HEAD_DIM) < (HEAD_DIM // 2)
    partner = jnp.where(first_half, pltpu.roll(x, w - HEAD_DIM // 2, 1), pltpu.roll(x, HEAD_DIM // 2, 1))
    reps = w // LANES
    if reps > 1:
        cos = jnp.concatenate([cos] * reps, axis=1)
        sin_signed = jnp.concatenate([sin_signed] * reps, axis=1)
    return x * cos + partner * sin_signed


def _inproj_body(x, g, w_ref, cos, sin_signed):
    h = _rms(x, g).astype(BF16)
    p = jnp.dot(h, w_ref[...], preferred_element_type=F32)
    o = 0
    q = _rope(p[:, o:o + ATT_W], cos, sin_signed) * (HEAD_DIM ** -0.5)
    o += ATT_W
    k = _rope(p[:, o:o + KV_W], cos, sin_signed)
    o += KV_W
    v = p[:, o:o + KV_W]
    o += KV_W
    gb = p[:, o:o + CONV_DIM]
    o += CONV_DIM
    gc = p[:, o:o + CONV_DIM]
    o += CONV_DIM
    hc = p[:, o:o + CONV_DIM]
    return q, k, v, gb, gc * hc


PROJ_OUT = ((ATT_W, BF16), (KV_W, F32), (KV_W, F32), (CONV_DIM, F32), (CONV_DIM, F32))


@dataclasses.dataclass(frozen=True)
class _StageCfg:
    n_main: int
    sample_in: str = ""
    sample_out: bool = False
    inproj: bool = False
    final: bool = False
    cast_next: bool = False
    ns: int = 4
    nseq: int = 128


def _stage_kernel(*refs, cfg):
    it = iter(refs)
    xm_ref, xa_ref = next(it), next(it)
    xs_ref = next(it) if cfg.sample_in else None
    g_ref, wg_ref, wu_ref, wd_ref = next(it), next(it), next(it), next(it)
    nxt_refs = [next(it) for _ in range(3)] if cfg.cast_next else []
    if cfg.inproj:
        g1_ref, win_ref, cosm_ref, sinm_ref, cosa_ref, sina_ref = (next(it) for _ in range(6))
    fg_ref = next(it) if cfg.final else None
    om_ref, oa_ref = next(it), next(it)
    os_ref = next(it) if cfg.sample_out else None
    if cfg.inproj:
        pm_refs = [next(it) for _ in PROJ_OUT]
        pa_refs = [next(it) for _ in PROJ_OUT]
    cast_refs = [next(it) for _ in range(3)] if cfg.cast_next else []
    x_scr = next(it) if (cfg.sample_in and not cfg.sample_out) else None

    i = pl.program_id(0)
    first_sample = cfg.n_main + 1 if cfg.sample_in else cfg.n_main

    for src, dst in zip(nxt_refs, cast_refs):
        dst[...] = src[...].astype(BF16)

    def run(x, store, tab_refs, p_refs):
        y = _ffn_body(x, g_ref[...], wg_ref, wu_ref, wd_ref)
        if cfg.final:
            y = _rms(y, fg_ref[...])
        store(y)
        if cfg.inproj:
            outs = _inproj_body(y, g1_ref[...], win_ref, tab_refs[0][...], tab_refs[1][...])
            for r, val in zip(p_refs, outs):
                r[...] = val.astype(r.dtype)

    def store_to(o_ref):
        def store(y):
            o_ref[...] = y
        return store

    def store_sample(y):
        for t in range(cfg.ns):
            os_ref[:, t, :] = y[t * cfg.nseq:(t + 1) * cfg.nseq]

    @pl.when(i == 0)
    def _():
        run(xm_ref[...], store_to(om_ref), (cosm_ref, sinm_ref) if cfg.inproj else None,
            pm_refs if cfg.inproj else None)

    if cfg.sample_out:
        @pl.when((i >= 1) & (i < first_sample))
        def _():
            run(xa_ref[...], store_to(oa_ref), None, None)

        @pl.when(i == first_sample)
        def _():
            run(xs_ref[...], store_sample, None, None)
        return

    if cfg.sample_in:
        @pl.when((i >= 1) & (i < first_sample))
        def _():
            x_scr[...] = xa_ref[...]

        @pl.when(i == first_sample)
        def _():
            if cfg.sample_in == "seq":
                for t in range(cfg.ns):
                    x_scr[t * cfg.nseq:(t + 1) * cfg.nseq, :] = xs_ref[:, t, :]
            else:
                x_scr[...] = xs_ref[...]

    @pl.when(i >= 1)
    def _():
        x = x_scr[...] if cfg.sample_in else xa_ref[...]
        run(x, store_to(oa_ref), (cosa_ref, sina_ref) if cfg.inproj else None, pa_refs if cfg.inproj else None)


CAST_ROWS_IN = 32
CAST_ROWS_FF = 128


def _stage_call(xm, xa, xs, gain, w_ffn, *, cfg, nxt=None, proj=None, final_gain=None, name="stage"):
    wg, wu, wd = w_ffn
    n_in = cfg.n_main
    n_out = cfg.n_main if cfg.sample_out else (cfg.n_main + 1 if cfg.sample_in else cfg.n_main)
    steps = 1 + (n_in + 1 if cfg.sample_in else n_in)
    assert xa.shape[0] == n_in * TM and xm.shape[0] == META_TILE

    def main_spec(w, n):
        return pl.BlockSpec((TM, w), lambda i: (jnp.clip(i - 1, 0, n - 1), 0))

    meta_spec = lambda w: _const_spec((META_TILE, w))
    in_specs = [meta_spec(D_MODEL), main_spec(D_MODEL, n_in)]
    args = [xm, xa]
    if cfg.sample_in:
        in_specs.append(_const_spec(xs.shape))
        args.append(xs)
    in_specs += [_const_spec((1, D_MODEL)), _const_spec((D_MODEL, D_FF)), _const_spec((D_MODEL, D_FF)),
                 _const_spec((D_FF, D_MODEL))]
    args += [gain.reshape(1, D_MODEL), wg, wu, wd]
    cast_shapes = ((D_MODEL, D_FF, CAST_ROWS_IN), (D_MODEL, D_FF, CAST_ROWS_IN), (D_FF, D_MODEL, CAST_ROWS_FF))
    if cfg.cast_next:
        stacks, layer, which = nxt
        for w4, (r, c, rb) in zip(stacks, cast_shapes):
            assert w4.shape[2:] == (r, c) and r % rb == 0 and r // rb <= steps
            in_specs.append(pl.BlockSpec((None, None, rb, c),
                                         lambda i, nblk=r // rb: (layer, which, jnp.minimum(i, nblk - 1), 0)))
            args.append(w4)
    if cfg.inproj:
        g1, w_in, cos_m, sin_m, cos_a, sin_a, tpb = proj
        n_tab = cos_a.shape[0] // TM
        tab = pl.BlockSpec((TM, LANES), lambda i: (jnp.where(i - 1 < cfg.n_main, jnp.maximum(i - 1, 0) % tpb, n_tab - 1), 0))
        in_specs += [_const_spec((1, D_MODEL)), _const_spec((D_MODEL, IN_W)), meta_spec(LANES), meta_spec(LANES), tab, tab]
        args += [g1.reshape(1, D_MODEL), w_in, cos_m, sin_m, cos_a, sin_a]
    if cfg.final:
        in_specs.append(_const_spec((1, D_MODEL)))
        args.append(final_gain.reshape(1, D_MODEL))

    out_specs = [meta_spec(D_MODEL), main_spec(D_MODEL, n_out)]
    out_shape = [jax.ShapeDtypeStruct((META_TILE, D_MODEL), F32), jax.ShapeDtypeStruct((n_out * TM, D_MODEL), F32)]
    if cfg.sample_out:
        out_specs.append(_const_spec((cfg.nseq, cfg.ns, D_MODEL)))
        out_shape.append(jax.ShapeDtypeStruct((cfg.nseq, cfg.ns, D_MODEL), F32))
    if cfg.inproj:
        out_specs += [meta_spec(w) for w, _ in PROJ_OUT] + [main_spec(w, n_out) for w, _ in PROJ_OUT]
        out_shape += [jax.ShapeDtypeStruct((META_TILE, w), dt) for w, dt in PROJ_OUT]
        out_shape += [jax.ShapeDtypeStruct((n_out * TM, w), dt) for w, dt in PROJ_OUT]
    if cfg.cast_next:
        for r, c, rb in cast_shapes:
            out_specs.append(pl.BlockSpec((rb, c), lambda i, nblk=r // rb: (jnp.minimum(i, nblk - 1), 0)))
            out_shape.append(jax.ShapeDtypeStruct((r, c), BF16))
    scratch = []
    if cfg.sample_in and not cfg.sample_out:
        scratch.append(pltpu.VMEM((TM, D_MODEL), F32))
    return pl.pallas_call(
        functools.partial(_stage_kernel, cfg=cfg),
        grid=(steps,),
        in_specs=in_specs,
        out_specs=out_specs,
        out_shape=out_shape,
        scratch_shapes=scratch,
        compiler_params=_params(),
        name=name,
    )(*args)


def _softmax_pv(s, bias, sinks, vv):
    r, kk = bias.shape
    lane = lax.broadcasted_iota(jnp.int32, (r, LANES), 1)
    p_rows, inv = [], []
    for c in range(GROUP):
        p_halves, inv_halves = [], []
        for h in range(N_KV_HEADS):
            sb = s[c * r:(c + 1) * r, h * kk:(h + 1) * kk] + bias
            snk = sinks[h][c]
            m = jnp.maximum(jnp.max(sb, axis=1, keepdims=True), snk)
            p = jnp.exp(sb - m)
            l = jnp.sum(p, axis=1, keepdims=True) + jnp.exp(snk - m)
            p_halves.append(p.astype(BF16))
            inv_halves.append(1.0 / l)
        p_rows.append(jnp.concatenate(p_halves, axis=1))
        inv.append(jnp.where(lane < HEAD_DIM, inv_halves[0], inv_halves[1]))
    o = jnp.dot(jnp.concatenate(p_rows, axis=0), vv, preferred_element_type=F32)
    return [o[c * r:(c + 1) * r] * inv[c] for c in range(GROUP)]


def _mix0_kernel(sink_ref, x_ref, q_ref, k_ref, v_ref, gb_ref, u_ref, hk_ref, hv_ref, hu_ref, cw_ref, wout_ref,
                 o_ref, ulast_ref, klast_ref, vlast_ref, k0h, k1h, v0h, v1h, ubuf, *, tm, kmin_first):
    i = pl.program_id(1)
    lane_h = lax.broadcasted_iota(jnp.int32, (BLOCK, LANES), 1) < HEAD_DIM

    @pl.when(i == 0)
    def _():
        hk = hk_ref[...]
        hv = hv_ref[...]
        k0h[...] = jnp.where(lane_h, hk, 0.0).astype(BF16)
        k1h[...] = jnp.where(lane_h, 0.0, hk).astype(BF16)
        v0h[...] = jnp.where(lane_h, hv, 0.0).astype(BF16)
        v1h[...] = jnp.where(lane_h, 0.0, hv).astype(BF16)
        ubuf[0:HALO_U, :] = hu_ref[...]

    lane_t = lax.broadcasted_iota(jnp.int32, (tm, LANES), 1) < HEAD_DIM
    kf = k_ref[...]
    vf = v_ref[...]
    k0 = jnp.concatenate([k0h[...], jnp.where(lane_t, kf, 0.0).astype(BF16)], axis=0)
    k1 = jnp.concatenate([k1h[...], jnp.where(lane_t, 0.0, kf).astype(BF16)], axis=0)
    v0 = jnp.concatenate([v0h[...], jnp.where(lane_t, vf, 0.0).astype(BF16)], axis=0)
    v1 = jnp.concatenate([v1h[...], jnp.where(lane_t, 0.0, vf).astype(BF16)], axis=0)
    k0h[...] = k0[tm:tm + BLOCK]
    k1h[...] = k1[tm:tm + BLOCK]
    v0h[...] = v0[tm:tm + BLOCK]
    v1h[...] = v1[tm:tm + BLOCK]
    klast_ref[...] = kf[tm - BLOCK:tm]
    vlast_ref[...] = vf[tm - BLOCK:tm]

    sinks = [[sink_ref[h * GROUP + c] for c in range(GROUP)] for h in range(N_KV_HEADS)]
    row = lax.broadcasted_iota(jnp.int32, (BLOCK, 2 * BLOCK), 0)
    col = lax.broadcasted_iota(jnp.int32, (BLOCK, 2 * BLOCK), 1)
    band = (col >= row) & (col <= row + WINDOW)
    bias_rest = jnp.where(band, 0.0, -jnp.inf)
    kmin = jnp.where(i == 0, kmin_first, 0)
    bias_first = jnp.where(band & (col >= kmin), 0.0, -jnp.inf)

    q = q_ref[...]
    att_blocks = []
    for j in range(tm // BLOCK):
        ks = slice(j * BLOCK, (j + 2) * BLOCK)
        kk = jnp.concatenate([k0[ks], k1[ks]], axis=0)
        vv = jnp.concatenate([v0[ks], v1[ks]], axis=0)
        qs = jnp.concatenate([q[j * BLOCK:(j + 1) * BLOCK, c * LANES:(c + 1) * LANES] for c in range(GROUP)], axis=0)
        s = lax.dot_general(qs, kk, (((1,), (1,)), ((), ())), preferred_element_type=F32)
        o = _softmax_pv(s, bias_first if j == 0 else bias_rest, sinks, vv)
        att_blocks.append(jnp.concatenate(o, axis=1))
    att = jnp.concatenate(att_blocks, axis=0) if len(att_blocks) > 1 else att_blocks[0]

    u = u_ref[...]
    ubuf[HALO_U:HALO_U + tm, :] = u
    cw = cw_ref[...]
    conv = cw[0:1] * ubuf[HALO_U - 2:HALO_U - 2 + tm, :] + cw[1:2] * ubuf[HALO_U - 1:HALO_U - 1 + tm, :] + cw[2:3] * u
    u_tail = u[tm - HALO_U:tm]
    ubuf[0:HALO_U, :] = u_tail
    ulast_ref[...] = u_tail

    mix_in = jnp.concatenate([att, gb_ref[...] * conv], axis=1).astype(BF16)
    o_ref[...] = x_ref[...] + jnp.dot(mix_in, wout_ref[...], preferred_element_type=F32)


def _mix0_call(sink, x, q, k, v, gb, u, hk, hv, hu, cw, wout, *, nbatch, tm, kmin_first, rows):
    per = rows // nbatch
    assert per % tm == 0 and tm % BLOCK == 0
    tpb = per // tm
    row = lambda w: pl.BlockSpec((tm, w), lambda b, i: (b * tpb + i, 0))
    return pl.pallas_call(
        functools.partial(_mix0_kernel, tm=tm, kmin_first=kmin_first),
        grid=(nbatch, tpb),
        in_specs=[pl.BlockSpec(memory_space=pltpu.SMEM),
                  row(D_MODEL), row(ATT_W), row(KV_W), row(KV_W), row(CONV_DIM), row(CONV_DIM),
                  _const_spec((BLOCK, KV_W)), _const_spec((BLOCK, KV_W)), _const_spec((HALO_U, CONV_DIM)),
                  _const_spec((CONV_W, CONV_DIM)), _const_spec((D_MODEL, D_MODEL))],
        out_specs=[row(D_MODEL), pl.BlockSpec((HALO_U, CONV_DIM), lambda b, i: (b, 0)),
                   pl.BlockSpec((BLOCK, KV_W), lambda b, i: (b, 0)), pl.BlockSpec((BLOCK, KV_W), lambda b, i: (b, 0))],
        out_shape=[jax.ShapeDtypeStruct((rows, D_MODEL), F32),
                   jax.ShapeDtypeStruct((nbatch * HALO_U, CONV_DIM), F32),
                   jax.ShapeDtypeStruct((nbatch * BLOCK, KV_W), F32),
                   jax.ShapeDtypeStruct((nbatch * BLOCK, KV_W), F32)],
        scratch_shapes=[pltpu.VMEM((BLOCK, KV_W), BF16)] * 4 + [pltpu.VMEM((tm + HALO_U, CONV_DIM), F32)],
        compiler_params=_params(),
        name="mix0",
    )(sink, x, q, k, v, gb, u, hk, hv, hu, cw, wout)


def _mix0s_kernel(sink_ref, x_ref, q_ref, k_ref, v_ref, gb_ref, u_ref, ck_ref, cv_ref, sc_ref, cw_ref, wout_ref,
                  o_ref, ko_ref, vo_ref, co_ref, qs_scr, os_scr, knew, vnew, *, sb, ns, nseq):
    ext = WINDOW + NEW_PAD
    base = pl.multiple_of(pl.program_id(0) * sb, sb)
    rows_t = lambda ref, t: ref[pl.ds(t * nseq + base, sb), :]
    lane_h = lax.broadcasted_iota(jnp.int32, (sb, LANES), 1) < HEAD_DIM
    for t in range(ns):
        qt = rows_t(q_ref, t).astype(F32)
        for c in range(GROUP):
            chunk = qt[:, c * LANES:(c + 1) * LANES]
            qs_scr[(0 * GROUP + c) * ns + t] = jnp.where(lane_h, chunk, 0.0)
            qs_scr[(1 * GROUP + c) * ns + t] = jnp.where(lane_h, 0.0, chunk)
        knew[t] = rows_t(k_ref, t)
        vnew[t] = rows_t(v_ref, t)
    nrow = N_KV_HEADS * GROUP * ns
    rid = lax.broadcasted_iota(jnp.int32, (nrow, ext), 0)
    e = lax.broadcasted_iota(jnp.int32, (nrow, ext), 1)
    t_of = rid % ns
    bias = jnp.where((e >= t_of) & (e <= t_of + WINDOW), 0.0, -jnp.inf)
    rid1 = lax.broadcasted_iota(jnp.int32, (nrow, 1), 0)
    snk = jnp.zeros((nrow, 1), F32)
    for hc in range(N_KV_HEADS * GROUP):
        snk = jnp.where(rid1 // ns == hc, sink_ref[hc], snk)
    zrows = jnp.zeros((NEW_PAD - ns, sb, LANES), F32)
    knew[ns:NEW_PAD] = zrows
    vnew[ns:NEW_PAD] = zrows
    for b in range(sb):
        kb = jnp.concatenate([ck_ref[b], knew[:, b, :]], axis=0)
        vb = jnp.concatenate([cv_ref[b], vnew[:, b, :]], axis=0)
        ko_ref[b] = kb[ns:ns + WINDOW]
        vo_ref[b] = vb[ns:ns + WINDOW]
        qb = qs_scr[:, b, :].astype(BF16)
        s = lax.dot_general(qb, kb.astype(BF16), (((1,), (1,)), ((), ())), preferred_element_type=F32) + bias
        m = jnp.maximum(jnp.max(s, axis=1, keepdims=True), snk)
        p = jnp.exp(s - m)
        l = jnp.sum(p, axis=1, keepdims=True) + jnp.exp(snk - m)
        o = jnp.dot(p.astype(BF16), vb.astype(BF16), preferred_element_type=F32)
        os_scr[:, b, :] = o * (1.0 / l)
    cw = cw_ref[...]
    ue = [sc_ref[r] for r in range(CONV_W - 1)] + [rows_t(u_ref, t) for t in range(ns)]
    rows = []
    for t in range(ns):
        att = jnp.concatenate(
            [jnp.where(lane_h, os_scr[(0 * GROUP + c) * ns + t], os_scr[(1 * GROUP + c) * ns + t]) for c in range(GROUP)],
            axis=1)
        conv = cw[0:1] * ue[t] + cw[1:2] * ue[t + 1] + cw[2:3] * ue[t + 2]
        rows.append(jnp.concatenate([att, rows_t(gb_ref, t) * conv], axis=1))
    mix_in = jnp.concatenate(rows, axis=0).astype(BF16)
    y = jnp.dot(mix_in, wout_ref[...], preferred_element_type=F32)
    for t in range(ns):
        o_ref[pl.ds(t * nseq + base, sb), :] = rows_t(x_ref, t) + y[t * sb:(t + 1) * sb]
    for r in range(CONV_W - 1):
        co_ref[r] = ue[ns + r]


def _mix0s_call(sink, x, q, k, v, gb, u, ck, cv, sc, cw, wout, *, sb, ns, tile):
    nseq = ck.shape[0]
    assert nseq % sb == 0 and ns * nseq == TM
    res = lambda w: pl.BlockSpec((TM, w), lambda i: (tile, 0), pipeline_mode=pl.Buffered(1))
    cache = pl.BlockSpec((sb, WINDOW, KV_W), lambda i: (i, 0, 0))
    cstate = pl.BlockSpec((CONV_W - 1, sb, CONV_DIM), lambda i: (0, i, 0))
    nrow = N_KV_HEADS * GROUP * ns
    return pl.pallas_call(
        functools.partial(_mix0s_kernel, sb=sb, ns=ns, nseq=nseq),
        grid=(nseq // sb,),
        in_specs=[pl.BlockSpec(memory_space=pltpu.SMEM),
                  res(D_MODEL), res(ATT_W), res(KV_W), res(KV_W), res(CONV_DIM), res(CONV_DIM),
                  cache, cache, cstate, _const_spec((CONV_W, CONV_DIM)), _const_spec((D_MODEL, D_MODEL))],
        out_specs=[pl.BlockSpec((TM, D_MODEL), lambda i: (0, 0)), cache, cache, cstate],
        out_shape=[jax.ShapeDtypeStruct((TM, D_MODEL), F32),
                   jax.ShapeDtypeStruct((nseq, WINDOW, KV_W), F32),
                   jax.ShapeDtypeStruct((nseq, WINDOW, KV_W), F32),
                   jax.ShapeDtypeStruct((CONV_W - 1, nseq, CONV_DIM), F32)],
        scratch_shapes=[pltpu.VMEM((nrow, sb, LANES), F32), pltpu.VMEM((nrow, sb, LANES), F32),
                        pltpu.VMEM((NEW_PAD, sb, KV_W), F32), pltpu.VMEM((NEW_PAD, sb, KV_W), F32)],
        compiler_params=_params(),
        name="mix0_sample",
    )(sink, x, q, k, v, gb, u, ck, cv, sc, cw, wout)


def _group_linear(p_groups, pw_ref, ps, x):
    z = jnp.concatenate(
        [jnp.dot(p.astype(BF16), pw_ref[g], preferred_element_type=F32) for g, p in enumerate(p_groups)], axis=1)
    return x + z * ps


assert POOL_WINDOWS == tuple(2 ** (g + 1) for g in range(len(POOL_WINDOWS)))
POOL_LO = 8
POOL_BASE = POOL_LO + HALO_P


def _window_sums(hbuf, lvl_bufs, tm):
    n = POOL_BASE + tm
    cur = hbuf[POOL_LO:n, :] + hbuf[POOL_LO - 1:n - 1, :]
    sums = [cur[HALO_P:, 0:POOL_GC]]
    w = 2
    for buf in lvl_bufs:
        rest = cur[:, POOL_GC:]
        buf[POOL_LO:n, :] = rest
        cur = rest + buf[POOL_LO - w:n - w, :]
        sums.append(cur[HALO_P:, 0:POOL_GC])
        w *= 2
    return sums


def _pool_kernel(x_ref, g_ref, hh_ref, pw_ref, ps_ref, o_ref, hlast_ref, hbuf, *lvl_bufs, tm, pos0):
    i = pl.program_id(1)

    @pl.when(i == 0)
    def _():
        hbuf[0:POOL_LO, :] = jnp.zeros((POOL_LO, D_MODEL), F32)
        for buf in lvl_bufs:
            buf[0:POOL_LO, :] = jnp.zeros((POOL_LO, buf.shape[1]), F32)
        hbuf[POOL_LO:POOL_BASE, :] = hh_ref[...]

    x = x_ref[...]
    h = _rms(x, g_ref[...])
    hbuf[POOL_BASE:POOL_BASE + tm, :] = h
    pos = pos0 + i * tm + lax.broadcasted_iota(jnp.int32, (tm, 1), 0)
    p_groups = []
    for g, (win, wsum) in enumerate(zip(POOL_WINDOWS, _window_sums(hbuf, lvl_bufs, tm))):
        cnt = jnp.clip(pos + 1, 1, win).astype(F32)
        p_groups.append(wsum * (1.0 / cnt) - h[:, g * POOL_GC:(g + 1) * POOL_GC])
    o_ref[...] = _group_linear(p_groups, pw_ref, ps_ref[...], x)
    tail = h[tm - HALO_P:tm]
    hbuf[POOL_LO:POOL_BASE, :] = tail
    hlast_ref[...] = tail


def _pool_call(x, gain, hh, pw, ps, *, nbatch, tm, pos0, rows):
    per = rows // nbatch
    assert per % tm == 0
    tpb = per // tm
    row = pl.BlockSpec((tm, D_MODEL), lambda b, i: (b * tpb + i, 0))
    ng = len(POOL_WINDOWS)
    return pl.pallas_call(
        functools.partial(_pool_kernel, tm=tm, pos0=pos0),
        grid=(nbatch, tpb),
        in_specs=[row, _const_spec((1, D_MODEL)), _const_spec((HALO_P, D_MODEL)),
                  _const_spec((ng, POOL_GC, POOL_GC)), _const_spec((1, D_MODEL))],
        out_specs=[row, pl.BlockSpec((HALO_P, D_MODEL), lambda b, i: (b, 0))],
        out_shape=[jax.ShapeDtypeStruct((rows, D_MODEL), F32),
                   jax.ShapeDtypeStruct((nbatch * HALO_P, D_MODEL), F32)],
        scratch_shapes=[pltpu.VMEM((POOL_BASE + tm, D_MODEL - g * POOL_GC), F32) for g in range(ng)],
        compiler_params=_params(),
        name="pool",
    )(x, gain.reshape(1, D_MODEL), hh, pw, ps.reshape(1, D_MODEL))


def _pools_kernel(x_ref, g_ref, sp_ref, pw_ref, ps_ref, o_ref, po_ref, *, sb, ns, nseq):
    base = pl.multiple_of(pl.program_id(0) * sb, sb)
    g = g_ref[...]
    xs = [x_ref[pl.ds(t * nseq + base, sb), :] for t in range(ns)]
    ext = [sp_ref[r] for r in range(POOL_HIST)] + [_rms(x, g) for x in xs]
    inv = [1.0 / win for win in POOL_WINDOWS]
    for t in range(ns):
        p_groups = []
        for gi, win in enumerate(POOL_WINDOWS):
            lanes = slice(gi * POOL_GC, (gi + 1) * POOL_GC)
            hg = ext[POOL_HIST + t][:, lanes]
            wsum = hg
            for d in range(1, win):
                wsum = wsum + ext[POOL_HIST + t - d][:, lanes]
            p_groups.append(wsum * inv[gi] - hg)
        o_ref[pl.ds(t * nseq + base, sb), :] = _group_linear(p_groups, pw_ref, ps_ref[...], xs[t])
    for r in range(POOL_HIST):
        po_ref[r] = ext[ns + r]


def _pools_call(x, gain, sp, pw, ps, *, sb, ns, tile):
    nseq = sp.shape[1]
    assert PAST_LEN + 1 >= max(POOL_WINDOWS) and nseq % sb == 0 and ns * nseq == TM
    state = pl.BlockSpec((POOL_HIST, sb, D_MODEL), lambda i: (0, i, 0))
    ng = len(POOL_WINDOWS)
    return pl.pallas_call(
        functools.partial(_pools_kernel, sb=sb, ns=ns, nseq=nseq),
        grid=(nseq // sb,),
        in_specs=[pl.BlockSpec((TM, D_MODEL), lambda i: (tile, 0), pipeline_mode=pl.Buffered(1)),
                  _const_spec((1, D_MODEL)), state, _const_spec((ng, POOL_GC, POOL_GC)), _const_spec((1, D_MODEL))],
        out_specs=[pl.BlockSpec((TM, D_MODEL), lambda i: (0, 0)), state],
        out_shape=[jax.ShapeDtypeStruct((TM, D_MODEL), F32),
                   jax.ShapeDtypeStruct((POOL_HIST, nseq, D_MODEL), F32)],
        compiler_params=_params(),
        name="pool_sample",
    )(x, gain.reshape(1, D_MODEL), sp, pw, ps.reshape(1, D_MODEL))


def _rope_tables(pos):
    half = HEAD_DIM // 2
    inv = ROPE_THETA ** (-np.arange(half, dtype=np.float64) / half)
    ang = np.asarray(pos, dtype=np.float64)[:, None] * inv[None, :]
    cos, sin = np.cos(ang), np.sin(ang)
    reps = LANES // HEAD_DIM
    cos_t = np.concatenate([cos, cos] * reps, axis=1).astype(np.float32)
    sin_t = np.concatenate([-sin, sin] * reps, axis=1).astype(np.float32)
    return jnp.asarray(cos_t), jnp.asarray(sin_t)


def _pair_heads(w, axis):
    shape = w.shape
    split = shape[:axis] + (N_KV_HEADS, GROUP, HEAD_DIM) + shape[axis + 1:]
    return jnp.swapaxes(w.reshape(split), axis, axis + 1).reshape(shape)


def kernel(x_prompt, x_sample, cache_k, cache_v, state_conv, state_pool, meta_tokens, ln_gain, ffn_w_gate, ffn_w_up,
           ffn_w_down, mix_w_in, attn_sink, conv_w, mix_w_out, pool_w, pool_scale, final_gain):
    nb, seq, _ = x_prompt.shape
    nseq, ns, _ = x_sample.shape
    assert ln_gain.shape[0] == 2 and meta_tokens.shape[0] == N_META and cache_k.shape[2] == WINDOW
    assert seq % TM == 0 and ns * nseq == TM
    n_prompt = nb * seq // TM
    tpb = seq // TM

    w_in = mix_w_in[0].astype(BF16)
    w_in = jnp.concatenate([_pair_heads(w_in[:, :ATT_W], 1), w_in[:, ATT_W:]], axis=1)
    w_out = mix_w_out[0].astype(BF16)
    w_out = jnp.concatenate([_pair_heads(w_out[:ATT_W], 0), w_out[ATT_W:]], axis=0)
    w_stacks = (ffn_w_gate, ffn_w_up, ffn_w_down)
    w_ffn = tuple(w[0, 0].astype(BF16) for w in w_stacks)
    pw = pool_w[0].astype(BF16)
    sink = attn_sink[0].reshape(N_Q_HEADS)
    cw = conv_w[0]

    cos_a, sin_a = _rope_tables(np.concatenate([N_META + np.arange(seq), np.repeat(PAST_LEN + np.arange(ns), nseq)]))
    cos_m, sin_m = _rope_tables(np.arange(META_TILE) - META_PAD)

    xp = x_prompt.reshape(nb * seq, D_MODEL)
    xm = jnp.concatenate([jnp.zeros((META_PAD, D_MODEL), F32), meta_tokens.astype(F32)], axis=0)

    cfg = _StageCfg(n_main=n_prompt, sample_in="seq", inproj=True, cast_next=True, ns=ns, nseq=nseq)
    outs = _stage_call(xm, xp, x_sample, ln_gain[0, 0], w_ffn, cfg=cfg, nxt=(w_stacks, 0, 1),
                       proj=(ln_gain[0, 1], w_in, cos_m, sin_m, cos_a, sin_a, tpb), name="ffn_inproj")
    xm, xa = outs[0], outs[1]
    qm, km, vm, gbm, um = outs[2:7]
    qa, ka, va, gba, ua = outs[7:12]
    w_ffn = outs[12:15]

    zk = jnp.zeros((BLOCK, KV_W), F32)
    zu = jnp.zeros((HALO_U, CONV_DIM), F32)
    xm, um_last, _, _ = _mix0_call(sink, xm, qm, km, vm, gbm, um, zk, zk, zu, cw, w_out,
                                   nbatch=1, tm=META_TILE, kmin_first=2 * BLOCK - N_META, rows=META_TILE)
    xp, up_last, kp_last, vp_last = _mix0_call(sink, xa, qa, ka, va, gba, ua, km, vm, um_last, cw, w_out,
                                               nbatch=nb, tm=TM, kmin_first=BLOCK - N_META, rows=nb * seq)
    xs, k_s, v_s, conv_s = _mix0s_call(
        sink, xa, qa, ka, va, gba, ua,
        cache_k[0].reshape(nseq, WINDOW, KV_W), cache_v[0].reshape(nseq, WINDOW, KV_W),
        jnp.swapaxes(state_conv[0], 0, 1), cw, w_out, sb=SEQ_BLOCK, ns=ns, tile=n_prompt)

    cfg = _StageCfg(n_main=n_prompt, sample_in="rows", cast_next=True, ns=ns, nseq=nseq)
    xm, xa, *w_ffn = _stage_call(xm, xp, xs, ln_gain[0, 2], w_ffn, cfg=cfg, nxt=(w_stacks, 1, 0), name="ffn_b0")

    cfg = _StageCfg(n_main=n_prompt + 1, cast_next=True, ns=ns, nseq=nseq)
    xm, xa, *w_ffn = _stage_call(xm, xa, None, ln_gain[1, 0], w_ffn, cfg=cfg, nxt=(w_stacks, 1, 1), name="ffn_a1")

    zh = jnp.zeros((HALO_P, D_MODEL), F32)
    xm, hm_last = _pool_call(xm, ln_gain[1, 1], zh, pw, pool_scale[0], nbatch=1, tm=META_TILE, pos0=-META_PAD,
                             rows=META_TILE)
    xp, hp_last = _pool_call(xa, ln_gain[1, 1], hm_last, pw, pool_scale[0], nbatch=nb, tm=TM, pos0=N_META,
                             rows=nb * seq)
    xs, pool_s = _pools_call(xa, ln_gain[1, 1], jnp.swapaxes(state_pool[0], 0, 1), pw, pool_scale[0],
                             sb=SEQ_BLOCK, ns=ns, tile=n_prompt)

    cfg = _StageCfg(n_main=n_prompt, sample_in="rows", sample_out=True, final=True, ns=ns, nseq=nseq)
    _, y_prompt, y_sample = _stage_call(xm, xp, xs, ln_gain[1, 2], w_ffn, cfg=cfg, final_gain=final_gain,
                                        name="ffn_final")

    kv_shape = (1, nb, WINDOW, N_KV_HEADS, HEAD_DIM)
    k_prompt = kp_last.reshape(kv_shape)
    v_prompt = vp_last.reshape(kv_shape)
    conv_prompt = up_last.reshape(1, nb, HALO_U, CONV_DIM)[:, :, HALO_U - (CONV_W - 1):]
    pool_prompt = hp_last.reshape(1, nb, HALO_P, D_MODEL)[:, :, HALO_P - POOL_HIST:]
    skv_shape = (1, nseq, WINDOW, N_KV_HEADS, HEAD_DIM)
    return (y_prompt.reshape(nb, seq, D_MODEL), y_sample, k_prompt, v_prompt, conv_prompt, pool_prompt,
            k_s.reshape(skv_shape), v_s.reshape(skv_shape),
            jnp.swapaxes(conv_s, 0, 1)[None], jnp.swapaxes(pool_s, 0, 1)[None])
```
